```python
import jax, jax.numpy as jnp
from jax import lax
import numpy as np

D_MODEL = 1024
BATCH = 8
SEQ = 4096
DEPTH = 1

PLE_DIM = 256
ATTN_HEADS = 8
HEAD_DIM = 64
ATTN_WIDTH = ATTN_HEADS * HEAD_DIM
IDX_HEADS = 4
IDX_DIM = 64
TOPK_MAX = 256
Q_BLOCK = 128
CONV_WIDTH = D_MODEL // 2
CONV_KERNEL = 31
MIX_WIDTH = ATTN_WIDTH + CONV_WIDTH
IN_SIZES = (ATTN_WIDTH, HEAD_DIM, HEAD_DIM, IDX_HEADS * IDX_DIM, IDX_DIM, IDX_HEADS, 2 * CONV_WIDTH)
IN_WIDTH = sum(IN_SIZES)
IN_SPLITS = tuple(int(s) for s in np.cumsum(IN_SIZES)[:-1])
N_EXPERTS = 32
TOP_K = 4
D_EXPERT = D_MODEL
SWIGLU_LIMIT = 7.0
SWIGLU_ALPHA = 1.702
MOE_BLOCK = 256
ROPE_THETA = 10000.0
EPS = 1e-6

kernel_name = 'hybrid_dsa_conformer_moe_block'


def rms_norm(x, g):
    xf = x.astype(jnp.float32)
    y = xf * lax.rsqrt(jnp.mean(xf * xf, axis=-1, keepdims=True) + EPS)
    return (y * g.astype(jnp.float32)).astype(x.dtype)


def layer_norm(x, g, b):
    xf = x.astype(jnp.float32)
    mu = jnp.mean(xf, axis=-1, keepdims=True)
    var = jnp.mean(jnp.square(xf - mu), axis=-1, keepdims=True)
    y = (xf - mu) * lax.rsqrt(var + EPS)
    return (y * g.astype(jnp.float32) + b.astype(jnp.float32)).astype(x.dtype)


def rope_tables(seq, dim):
    inv = ROPE_THETA ** (-jnp.arange(0, dim, 2, dtype=jnp.float32) / dim)
    ang = jnp.arange(seq, dtype=jnp.float32)[:, None] * inv[None, :]
    return jnp.cos(ang), jnp.sin(ang)


def apply_rope(x, cos, sin):
    extra = x.ndim - 3
    c = cos.reshape((cos.shape[0],) + (1,) * extra + (cos.shape[1],))
    s = sin.reshape((sin.shape[0],) + (1,) * extra + (sin.shape[1],))
    x1, x2 = jnp.split(x.astype(jnp.float32), 2, axis=-1)
    return jnp.concatenate([x1 * c - x2 * s, x2 * c + x1 * s], axis=-1).astype(x.dtype)


def dsa_attention(q, k, v, q_idx, k_idx, w_idx):
    B, S = q.shape[0], q.shape[1]
    n_sel = min(TOPK_MAX, S // 4)
    nb = S // Q_BLOCK
    key_pos = jnp.arange(S, dtype=jnp.int32)
    k_idx_f = k_idx.astype(jnp.float32)

    def to_blocks(a):
        return a.reshape((B, nb, Q_BLOCK) + a.shape[2:]).swapaxes(0, 1)

    def one_block(args):
        qb, qib, wb, tb = args
        s = jnp.einsum('bqhd,bsd->bqhs', qib.astype(jnp.float32), k_idx_f) * (IDX_DIM ** -0.5)
        score = jnp.einsum('bqhs,bqh->bqs', jax.nn.relu(s), wb.astype(jnp.float32)) * (IDX_HEADS ** -0.5)
        causal = key_pos[None, :] <= tb[:, None]
        score = jnp.where(causal[None], score, -jnp.inf)
        _, idx = lax.top_k(score, n_sel)
        valid = idx <= tb[None, :, None]
        k_sel = jax.vmap(lambda kb, ib: kb[ib])(k, idx)
        v_sel = jax.vmap(lambda vb, ib: vb[ib])(v, idx)
        logits = jnp.einsum('bqhd,bqkd->bqhk', qb, k_sel).astype(jnp.float32) * (HEAD_DIM ** -0.5)
        logits = jnp.where(valid[:, :, None, :], logits, -jnp.inf)
        probs = jax.nn.softmax(logits, axis=-1).astype(v.dtype)
        return jnp.einsum('bqhk,bqkd->bqhd', probs, v_sel)

    pos_blocks = key_pos.reshape(nb, Q_BLOCK)
    out = lax.map(one_block, (to_blocks(q), to_blocks(q_idx), to_blocks(w_idx), pos_blocks))
    return out.swapaxes(0, 1).reshape(B, S, ATTN_HEADS * HEAD_DIM)


def conformer_conv(u, glu_b, dw_w, dw_b, ln_g, ln_b):
    a, g = jnp.split(u + glu_b, 2, axis=-1)
    h = a * jax.nn.sigmoid(g)
    h = lax.conv_general_dilated(
        h, dw_w[:, None, :].astype(h.dtype), window_strides=(1,),
        padding=[(CONV_KERNEL - 1, 0)], dimension_numbers=('NWC', 'WIO', 'NWC'),
        feature_group_count=CONV_WIDTH) + dw_b
    h = layer_norm(h, ln_g, ln_b)
    return jax.nn.silu(h)


def moe(x, w_router, b_router, w_up, b_up, w_down, b_down):
    N = x.shape[0]
    logits = (x @ w_router).astype(jnp.float32) + b_router.astype(jnp.float32)
    top_v, top_e = lax.top_k(logits, TOP_K)
    gates = jax.nn.softmax(top_v, axis=-1)
    A = N * TOP_K
    e_flat = top_e.reshape(A)
    tok_flat = jnp.arange(A, dtype=jnp.int32) // TOP_K
    gate_flat = gates.reshape(A)
    order = jnp.argsort(e_flat)
    e_sorted = e_flat[order]
    counts = jnp.bincount(e_flat, length=N_EXPERTS)
    padded = (counts + MOE_BLOCK - 1) // MOE_BLOCK * MOE_BLOCK
    start = jnp.cumsum(counts) - counts
    cum_padded = jnp.cumsum(padded)
    pstart = cum_padded - padded
    dest = pstart[e_sorted] + jnp.arange(A, dtype=jnp.int32) - start[e_sorted]
    n_blocks = (A + N_EXPERTS * (MOE_BLOCK - 1) + MOE_BLOCK - 1) // MOE_BLOCK
    P = n_blocks * MOE_BLOCK
    row_tok = jnp.zeros((P,), jnp.int32).at[dest].set(tok_flat[order])
    row_gate = jnp.zeros((P,), jnp.float32).at[dest].set(gate_flat[order])
    block_start = jnp.arange(n_blocks, dtype=jnp.int32) * MOE_BLOCK
    block_exp = jnp.minimum(jnp.searchsorted(cum_padded, block_start, side='right'), N_EXPERTS - 1)

    def expert_block(args):
        e, toks = args
        h = x[toks] @ w_up[e] + b_up[e]
        gt, lin = jnp.split(h, 2, axis=-1)
        gt = jnp.minimum(gt, SWIGLU_LIMIT)
        lin = jnp.clip(lin, -SWIGLU_LIMIT, SWIGLU_LIMIT)
        act = gt * jax.nn.sigmoid(SWIGLU_ALPHA * gt) * (lin + 1.0)
        return act @ w_down[e] + b_down[e]

    y_rows = lax.map(expert_block, (block_exp, row_tok.reshape(n_blocks, MOE_BLOCK)))
    y_rows = y_rows.reshape(P, x.shape[1]).astype(jnp.float32) * row_gate[:, None]
    return jax.ops.segment_sum(y_rows, row_tok, num_segments=N).astype(x.dtype)


def setup_inputs(seed: int = 0) -> dict:
    key = jax.random.key(seed)
    ks = jax.random.split(key, 21)
    f32 = jnp.float32

    def nrm(k, shape, scale):
        return jax.random.normal(k, shape, f32) * scale

    def gain(k, shape):
        return 1.0 + 0.05 * jax.random.normal(k, shape, f32)

    return {
        'x': nrm(ks[0], (BATCH, SEQ, D_MODEL), 1.0),
        'p': nrm(ks[1], (DEPTH, BATCH, SEQ, PLE_DIM), 1.0),
        'w_in': nrm(ks[2], (DEPTH, D_MODEL, IN_WIDTH), D_MODEL ** -0.5),
        'w_out': nrm(ks[3], (DEPTH, MIX_WIDTH, D_MODEL), MIX_WIDTH ** -0.5),
        'g_mix': gain(ks[4], (DEPTH, D_MODEL)),
        'g_ffn': gain(ks[5], (DEPTH, D_MODEL)),
        'conv_glu_b': nrm(ks[6], (DEPTH, 2 * CONV_WIDTH), 0.02),
        'conv_dw_w': nrm(ks[7], (DEPTH, CONV_KERNEL, CONV_WIDTH), CONV_KERNEL ** -0.5),
        'conv_dw_b': nrm(ks[8], (DEPTH, CONV_WIDTH), 0.02),
        'conv_ln_g': gain(ks[9], (DEPTH, CONV_WIDTH)),
        'conv_ln_b': nrm(ks[10], (DEPTH, CONV_WIDTH), 0.02),
        'w_router': nrm(ks[11], (DEPTH, D_MODEL, N_EXPERTS), D_MODEL ** -0.5),
        'b_router': nrm(ks[12], (DEPTH, N_EXPERTS), 0.01),
        'w_up': nrm(ks[13], (DEPTH, N_EXPERTS, D_MODEL, 2 * D_EXPERT), D_MODEL ** -0.5),
        'b_up': nrm(ks[14], (DEPTH, N_EXPERTS, 2 * D_EXPERT), 0.02),
        'w_down': nrm(ks[15], (DEPTH, N_EXPERTS, D_EXPERT, D_MODEL), D_EXPERT ** -0.5),
        'b_down': nrm(ks[16], (DEPTH, N_EXPERTS, D_MODEL), 0.02),
        'w_ple_proj': nrm(ks[17], (DEPTH, PLE_DIM, D_MODEL), PLE_DIM ** -0.5),
        'w_ple_gate': nrm(ks[18], (DEPTH, D_MODEL, D_MODEL), D_MODEL ** -0.5),
        'g_ple': gain(ks[19], (DEPTH, D_MODEL)),
        'g_final': gain(ks[20], (D_MODEL,)),
    }


def reference(x, p, w_in, w_out, g_mix, g_ffn, conv_glu_b, conv_dw_w, conv_dw_b, conv_ln_g,
              conv_ln_b, w_router, b_router, w_up, b_up, w_down, b_down, w_ple_proj,
              w_ple_gate, g_ple, g_final):
    B, S, D = x.shape
    cos_a, sin_a = rope_tables(S, HEAD_DIM)
    cos_i, sin_i = rope_tables(S, IDX_DIM)
    h = x
    for i in range(DEPTH):
        u = rms_norm(h, g_mix[i]) @ w_in[i]
        q, k, v, qi, ki, wi, cu = jnp.split(u, IN_SPLITS, axis=-1)
        q = apply_rope(q.reshape(B, S, ATTN_HEADS, HEAD_DIM), cos_a, sin_a)
        k = apply_rope(k, cos_a, sin_a)
        qi = apply_rope(qi.reshape(B, S, IDX_HEADS, IDX_DIM), cos_i, sin_i)
        ki = apply_rope(ki, cos_i, sin_i)
        attn = dsa_attention(q, k, v, qi, ki, wi)
        conv = conformer_conv(cu, conv_glu_b[i], conv_dw_w[i], conv_dw_b[i],
                              conv_ln_g[i], conv_ln_b[i])
        h = h + jnp.concatenate([attn, conv], axis=-1) @ w_out[i]
        y = moe(rms_norm(h, g_ffn[i]).reshape(B * S, D), w_router[i], b_router[i],
                w_up[i], b_up[i], w_down[i], b_down[i])
        h = h + y.reshape(B, S, D)
        gate = jax.nn.sigmoid((rms_norm(h, g_ple[i]) @ w_ple_gate[i]).astype(jnp.float32))
        h = h + ((p[i] @ w_ple_proj[i]).astype(jnp.float32) * gate).astype(h.dtype)
    return rms_norm(h, g_final)
```

```python
import functools

import jax
import jax.numpy as jnp
import numpy as np
from jax import lax
from jax.experimental import pallas as pl
from jax.experimental.pallas import tpu as pltpu

F32 = jnp.float32
BF16 = jnp.bfloat16
I32 = jnp.int32

ATTN_HEADS = 8
HEAD_DIM = 64
IDX_HEADS = 4
IDX_DIM = 64
TOPK_MAX = 256
CONV_KERNEL = 31
N_EXPERTS = 32
TOP_K = 4
SWIGLU_LIMIT = 7.0
SWIGLU_ALPHA = 1.702
ROPE_THETA = 10000.0
EPS = 1e-6

LANES = 128
CONV_HALO = 32
NEG_BIG = -1e30
INT_MIN = -2147483648
VMEM_LIMIT = 56 * 1024 * 1024


def _params(n_axes):
    return pltpu.CompilerParams(dimension_semantics=("arbitrary",) * n_axes,
                                vmem_limit_bytes=VMEM_LIMIT)


def _rms(x, g):
    return x * lax.rsqrt(jnp.mean(x * x, axis=-1, keepdims=True) + EPS) * g


def _inproj_kernel(x_ref, g_ref, w_ref, cos_ref, sin_ref, glub_ref, dww_ref, dwb_ref, lng_ref, lnb_ref,
                   q_ref, qi_ref, k_ref, ki_ref, v_ref, wi_ref, conv_ref, hbuf, *, tm, conv_w, row_blk):
    t = pl.program_id(1)
    xn = _rms(x_ref[...], g_ref[...]).astype(BF16)
    u = jnp.dot(xn, w_ref[...], preferred_element_type=F32)
    rope_w = (ATTN_HEADS + IDX_HEADS + 2) * HEAD_DIM
    cos = cos_ref[...]
    sin = sin_ref[...]
    r = [u[:, c * LANES:(c + 1) * LANES] * cos + u[:, rope_w + c * LANES:rope_w + (c + 1) * LANES] * sin
         for c in range(rope_w // LANES)]
    for h in range(ATTN_HEADS):
        blk = r[h // 2][:, (h % 2) * HEAD_DIM:(h % 2 + 1) * HEAD_DIM]
        q_ref[h] = (blk * (HEAD_DIM ** -0.5)).astype(BF16)
    for h in range(IDX_HEADS):
        blk = r[ATTN_HEADS // 2 + h // 2][:, (h % 2) * IDX_DIM:(h % 2 + 1) * IDX_DIM]
        qi_ref[h] = blk.astype(BF16)
    last = r[(ATTN_HEADS + IDX_HEADS) // 2]
    k_ref[...] = last[:, :HEAD_DIM].astype(BF16)
    ki_ref[...] = last[:, HEAD_DIM:].astype(BF16)
    ub = u[:, 2 * rope_w:2 * rope_w + LANES]
    wi_ref[...] = ub
    v_ref[...] = ub[:, LANES - HEAD_DIM:].astype(BF16)

    cu = u[:, 2 * rope_w + LANES:] + glub_ref[...]
    hh = cu[:, :conv_w] * jax.nn.sigmoid(cu[:, conv_w:])

    @pl.when(t == 0)
    def _():
        hbuf[0:CONV_HALO, :] = jnp.zeros((CONV_HALO, conv_w), F32)

    hbuf[CONV_HALO:CONV_HALO + tm, :] = hh
    base = CONV_HALO - (CONV_KERNEL - 1)
    for rb in range(tm // row_blk):
        acc = jnp.zeros((row_blk, conv_w), F32)
        for j in range(CONV_KERNEL):
            s0 = rb * row_blk + base + j
            acc = acc + hbuf[s0:s0 + row_blk, :] * dww_ref[j:j + 1, :]
        y = acc + dwb_ref[...]
        mu = jnp.mean(y, axis=-1, keepdims=True)
        yc = y - mu
        var = jnp.mean(yc * yc, axis=-1, keepdims=True)
        z = yc * lax.rsqrt(var + EPS) * lng_ref[...] + lnb_ref[...]
        conv_ref[rb * row_blk:(rb + 1) * row_blk, :] = (z * jax.nn.sigmoid(z)).astype(BF16)
    hbuf[0:CONV_HALO, :] = hbuf[tm:tm + CONV_HALO, :]


def _inproj(x, g_mix, w_all, cos_t, sin_t, glu_b, dw_w, dw_b, ln_g, ln_b, tm):
    B, S, D = x.shape
    conv_w = dw_w.shape[1]
    nt = S // tm
    wtot = w_all.shape[1]
    row_blk = 64 if tm % 64 == 0 else tm
    kern = functools.partial(_inproj_kernel, tm=tm, conv_w=conv_w, row_blk=row_blk)
    full = lambda shape: pl.BlockSpec(shape, lambda b, t: (0,) * len(shape))
    out_shape = (
        jax.ShapeDtypeStruct((B, ATTN_HEADS, S, HEAD_DIM), BF16),
        jax.ShapeDtypeStruct((B, IDX_HEADS, S, IDX_DIM), BF16),
        jax.ShapeDtypeStruct((B, S, HEAD_DIM), BF16),
        jax.ShapeDtypeStruct((B, S, IDX_DIM), BF16),
        jax.ShapeDtypeStruct((B, S, HEAD_DIM), BF16),
        jax.ShapeDtypeStruct((B, S, LANES), F32),
        jax.ShapeDtypeStruct((B, S, conv_w), BF16),
    )
    out_specs = (
        pl.BlockSpec((None, ATTN_HEADS, tm, HEAD_DIM), lambda b, t: (b, 0, t, 0)),
        pl.BlockSpec((None, IDX_HEADS, tm, IDX_DIM), lambda b, t: (b, 0, t, 0)),
        pl.BlockSpec((None, tm, HEAD_DIM), lambda b, t: (b, t, 0)),
        pl.BlockSpec((None, tm, IDX_DIM), lambda b, t: (b, t, 0)),
        pl.BlockSpec((None, tm, HEAD_DIM), lambda b, t: (b, t, 0)),
        pl.BlockSpec((None, tm, LANES), lambda b, t: (b, t, 0)),
        pl.BlockSpec((None, tm, conv_w), lambda b, t: (b, t, 0)),
    )
    in_specs = [
        pl.BlockSpec((None, tm, D), lambda b, t: (b, t, 0)),
        full((1, D)),
        full((D, wtot)),
        pl.BlockSpec((tm, LANES), lambda b, t: (t, 0)),
        pl.BlockSpec((tm, LANES), lambda b, t: (t, 0)),
        full((1, 2 * conv_w)),
        full((CONV_KERNEL, conv_w)),
        full((1, conv_w)),
        full((1, conv_w)),
        full((1, conv_w)),
    ]
    return pl.pallas_call(
        kern, grid=(B, nt), in_specs=in_specs, out_specs=out_specs, out_shape=out_shape,
        scratch_shapes=[pltpu.VMEM((CONV_HALO + tm, conv_w), F32)],
        compiler_params=_params(2), name="inproj_rope_conv",
    )(x, g_mix, w_all, cos_t, sin_t, glu_b, dw_w, dw_b, ln_g, ln_b)


def _lane_rep(col):
    return jnp.broadcast_to(col, (col.shape[0], LANES))


def _count(mask_fn, key_scr, nk, qb, sc):
    def chunk_body(c, acc):
        keys = key_scr[c]
        for g in range(sc // LANES):
            m = mask_fn(keys[:, g * LANES:(g + 1) * LANES], c, g)
            acc = acc + jnp.where(m, 1.0, 0.0)
        return acc
    acc = lax.fori_loop(0, nk, chunk_body, jnp.zeros((qb, LANES), F32))
    return _lane_rep(jnp.sum(acc, axis=-1, keepdims=True))


def _dsa_kernel(q_ref, qi_ref, wi_ref, k_ref, ki_ref, v_ref, o_ref, key_scr, m_scr, l_scr, acc_scr,
                *, qb, sc, n_sel, idx_bits):
    j = pl.program_id(1)
    q_lo = j * qb
    nk = (q_lo + qb + sc - 1) // sc
    n_sel_f = jnp.float32(n_sel)
    row_pos = q_lo + lax.broadcasted_iota(I32, (qb, LANES), 0)
    lane_id = lax.broadcasted_iota(I32, (qb, LANES), 1)

    qi = qi_ref[...]
    w_eff = wi_ref[...] * (IDX_DIM ** -0.5 * IDX_HEADS ** -0.5)
    w_rep = [_lane_rep(w_eff[:, h:h + 1]) for h in range(IDX_HEADS)]

    def score_body(c, carry):
        kc = ki_ref[pl.ds(pl.multiple_of(c * sc, sc), sc), :]
        parts = []
        for g in range(sc // LANES):
            parts.append(jnp.zeros((qb, LANES), F32))
        for h in range(IDX_HEADS):
            s = lax.dot_general(qi[h], kc, (((1,), (1,)), ((), ())), preferred_element_type=F32)
            s = jnp.maximum(s, 0.0)
            for g in range(sc // LANES):
                parts[g] = parts[g] + s[:, g * LANES:(g + 1) * LANES] * w_rep[h]
        for g in range(sc // LANES):
            col = c * sc + g * LANES + lane_id
            sco = parts[g]
            sco = jnp.where(sco == 0.0, 0.0, sco)
            sco = jnp.where(col <= row_pos, sco, -jnp.inf)
            bits = pltpu.bitcast(sco, I32)
            parts[g] = jnp.where(bits < 0, bits ^ jnp.int32(0x7FFFFFFF), bits)
        key_scr[c] = jnp.concatenate(parts, axis=-1)
        return carry

    lax.fori_loop(0, nk, score_body, 0)

    cnt0 = _count(lambda kg, c, g: kg >= 0, key_scr, nk, qb, sc)
    thr = jnp.where(cnt0 >= n_sel_f, jnp.int32(0), jnp.int32(INT_MIN))

    def bit_body(i, thr):
        cand = thr | lax.shift_left(jnp.int32(1), jnp.int32(30) - i)
        cnt = _count(lambda kg, c, g: kg >= cand, key_scr, nk, qb, sc)
        return jnp.where(cnt >= n_sel_f, cand, thr)

    thr = lax.fori_loop(0, 31, bit_body, thr)

    cnt_gt = _count(lambda kg, c, g: kg > thr, key_scr, nk, qb, sc)
    need = n_sel_f - cnt_gt

    def idx_body(i, jj):
        cand = jj | lax.shift_left(jnp.int32(1), jnp.int32(idx_bits - 1) - i)

        def m(kg, c, g):
            col = c * sc + g * LANES + lane_id
            return jnp.logical_and(kg == thr, col < cand)
        cnt = _count(m, key_scr, nk, qb, sc)
        return jnp.where(cnt < need, cand, jj)

    jmax = lax.fori_loop(0, idx_bits, idx_body, jnp.zeros((qb, LANES), I32))

    nh = ATTN_HEADS
    q2 = q_ref[...].reshape(nh * qb, HEAD_DIM)
    m_scr[...] = jnp.full((nh * qb, LANES), NEG_BIG, F32)
    l_scr[...] = jnp.zeros((nh * qb, LANES), F32)
    acc_scr[...] = jnp.zeros((nh * qb, HEAD_DIM), F32)

    def attn_body(c, carry):
        off = pl.multiple_of(c * sc, sc)
        kc = k_ref[pl.ds(off, sc), :]
        vc = v_ref[pl.ds(off, sc), :]
        keys = key_scr[c]
        bias = []
        for g in range(sc // LANES):
            kg = keys[:, g * LANES:(g + 1) * LANES]
            col = c * sc + g * LANES + lane_id
            tie = jnp.logical_and(kg == thr, col <= jmax)
            sel = jnp.logical_and(jnp.logical_or(kg > thr, tie), col <= row_pos)
            bias.append(jnp.where(sel, 0.0, NEG_BIG))
        bias = jnp.concatenate(bias, axis=-1)
        s = lax.dot_general(q2, kc, (((1,), (1,)), ((), ())), preferred_element_type=F32)
        s = (s.reshape(nh, qb, sc) + bias[None]).reshape(nh * qb, sc)
        m_old = m_scr[...]
        m_new = jnp.maximum(m_old, _lane_rep(jnp.max(s, axis=-1, keepdims=True)))
        alpha = jnp.exp(m_old - m_new)
        p = jnp.exp(s - jnp.tile(m_new, (1, sc // LANES)))
        l_scr[...] = alpha * l_scr[...] + _lane_rep(jnp.sum(p, axis=-1, keepdims=True))
        pv = jnp.dot(p.astype(BF16), vc, preferred_element_type=F32)
        acc_scr[...] = alpha[:, :HEAD_DIM] * acc_scr[...] + pv
        m_scr[...] = m_new
        return carry

    lax.fori_loop(0, nk, attn_body, 0)
    out = acc_scr[...] / l_scr[...][:, :HEAD_DIM]
    o_ref[...] = out.reshape(nh, qb, HEAD_DIM).astype(BF16)


def _dsa(q, qi, wi, k, ki, v, qb, sc):
    B, H, S, dh = q.shape
    n_sel = min(TOPK_MAX, S // 4)
    idx_bits = max(1, int(np.ceil(np.log2(S))))
    assert S % sc == 0 and sc % qb == 0 and sc % LANES == 0
    kern = functools.partial(_dsa_kernel, qb=qb, sc=sc, n_sel=n_sel, idx_bits=idx_bits)
    in_specs = [
        pl.BlockSpec((None, H, qb, dh), lambda b, j: (b, 0, j, 0)),
        pl.BlockSpec((None, IDX_HEADS, qb, IDX_DIM), lambda b, j: (b, 0, j, 0)),
        pl.BlockSpec((None, qb, LANES), lambda b, j: (b, j, 0)),
        pl.BlockSpec((None, S, dh), lambda b, j: (b, 0, 0)),
        pl.BlockSpec((None, S, IDX_DIM), lambda b, j: (b, 0, 0)),
        pl.BlockSpec((None, S, dh), lambda b, j: (b, 0, 0)),
    ]
    return pl.pallas_call(
        kern, grid=(B, S // qb), in_specs=in_specs,
        out_specs=pl.BlockSpec((None, H, qb, dh), lambda b, j: (b, 0, j, 0)),
        out_shape=jax.ShapeDtypeStruct((B, H, S, dh), BF16),
        scratch_shapes=[
            pltpu.VMEM((S // sc, qb, sc), I32),
            pltpu.VMEM((H * qb, LANES), F32),
            pltpu.VMEM((H * qb, LANES), F32),
            pltpu.VMEM((H * qb, dh), F32),
        ],
        compiler_params=_params(2), name="dsa_attention",
    )(q, qi, wi, k, ki, v)


def _mix_router_kernel(x_ref, attn_ref, conv_ref, wout_ref, g_ref, wr_ref, br_ref,
                       h1_ref, xn_ref, route_ref, gate_ref, cnt_ref, carry, *, tm):
    i = pl.program_id(0)

    @pl.when(i == 0)
    def _():
        carry[...] = jnp.zeros(carry.shape, F32)

    attn_w = ATTN_HEADS * HEAD_DIM
    acc = x_ref[...] + jnp.dot(conv_ref[...], wout_ref[attn_w:, :], preferred_element_type=F32)
    for h in range(ATTN_HEADS):
        acc = acc + jnp.dot(attn_ref[h], wout_ref[h * HEAD_DIM:(h + 1) * HEAD_DIM, :],
                            preferred_element_type=F32)
    h1_ref[...] = acc
    xn = _rms(acc, g_ref[...])
    xn_ref[...] = xn
    logits = jnp.dot(xn.astype(BF16), wr_ref[...], preferred_element_type=F32) + br_ref[...]

    ne = logits.shape[1]
    eid = lax.broadcasted_iota(I32, (tm, ne), 1)
    work = logits
    vals, hots = [], []
    for _ in range(TOP_K):
        m = jnp.max(work, axis=-1, keepdims=True)
        e = jnp.min(jnp.where(work == m, eid, ne), axis=-1, keepdims=True)
        hot = eid == e
        work = jnp.where(hot, -jnp.inf, work)
        vals.append(m)
        hots.append(hot)
    ex = [jnp.exp(v - vals[0]) for v in vals]
    den = ex[0] + ex[1] + ex[2] + ex[3]

    multi = jnp.zeros((tm, ne), F32)
    for hot in hots:
        multi = multi + jnp.where(hot, 1.0, 0.0)
    ri = lax.broadcasted_iota(I32, (tm, tm), 0)
    ci = lax.broadcasted_iota(I32, (tm, tm), 1)
    tri = jnp.where(ci < ri, 1.0, 0.0).astype(BF16)
    before = jnp.dot(tri, multi.astype(BF16), preferred_element_type=F32) + carry[0:1, :ne]
    carry[0:1, :ne] = carry[0:1, :ne] + jnp.sum(multi, axis=0, keepdims=True)
    cnt_ref[...] = carry[...]

    lane8 = lax.broadcasted_iota(I32, (tm, 2 * TOP_K), 1)
    lane4 = lax.broadcasted_iota(I32, (tm, TOP_K), 1)
    route = jnp.zeros((tm, 2 * TOP_K), I32)
    gates = jnp.zeros((tm, TOP_K), F32)
    for kk in range(TOP_K):
        e = jnp.sum(jnp.where(hots[kk], eid, 0), axis=-1, keepdims=True)
        rk = jnp.sum(jnp.where(hots[kk], before, 0.0), axis=-1, keepdims=True).astype(I32)
        route = route + jnp.where(lane8 == kk, e, 0) + jnp.where(lane8 == TOP_K + kk, rk, 0)
        gates = gates + jnp.where(lane4 == kk, ex[kk] / den, 0.0)
    route_ref[...] = route
    gate_ref[...] = gates


def _mix_router(x2, attn, conv, w_out, g_ffn, w_router, b_router, tm):
    N, D = x2.shape
    B, H, S, dh = attn.shape
    nt = S // tm
    ne = w_router.shape[1]
    cw = conv.shape[-1]
    kern = functools.partial(_mix_router_kernel, tm=tm)
    full = lambda shape: pl.BlockSpec(shape, lambda i: (0,) * len(shape))
    in_specs = [
        pl.BlockSpec((tm, D), lambda i: (i, 0)),
        pl.BlockSpec((None, H, tm, dh), lambda i: (i // nt, 0, i % nt, 0)),
        pl.BlockSpec((None, tm, cw), lambda i: (i // nt, i % nt, 0)),
        full(w_out.shape), full((1, D)), full((D, ne)), full((1, ne)),
    ]
    out_shape = (
        jax.ShapeDtypeStruct((N, D), F32),
        jax.ShapeDtypeStruct((N, D), F32),
        jax.ShapeDtypeStruct((N, 2 * TOP_K), I32),
        jax.ShapeDtypeStruct((N, TOP_K), F32),
        jax.ShapeDtypeStruct((8, LANES), F32),
    )
    out_specs = (
        pl.BlockSpec((tm, D), lambda i: (i, 0)),
        pl.BlockSpec((tm, D), lambda i: (i, 0)),
        pl.BlockSpec((tm, 2 * TOP_K), lambda i: (i, 0)),
        pl.BlockSpec((tm, TOP_K), lambda i: (i, 0)),
        pl.BlockSpec((8, LANES), lambda i: (0, 0)),
    )
    return pl.pallas_call(
        kern, grid=(N // tm,), in_specs=in_specs, out_specs=out_specs, out_shape=out_shape,
        scratch_shapes=[pltpu.VMEM((8, LANES), F32)],
        compiler_params=_params(1), name="outproj_router",
    )(x2, attn, conv, w_out, g_ffn, w_router, b_router)


def _row_copy(src_ref, src_row, dst_ref, dst_row, sem):
    return pltpu.make_async_copy(src_ref.at[pl.ds(src_row, 1)], dst_ref.at[pl.ds(dst_row, 1)], sem)


def _dispatch_kernel(dest_ref, x_ref, init_ref, out_ref, sem, *, tb):
    del init_ref

    def start(r, c):
        for kk in range(TOP_K):
            _row_copy(x_ref, r, out_ref, dest_ref[r * TOP_K + kk], sem).start()
        return c

    lax.fori_loop(0, tb, start, 0)

    def wait(r, c):
        for kk in range(TOP_K):
            _row_copy(x_ref, 0, out_ref, 0, sem).wait()
        return c

    lax.fori_loop(0, tb, wait, 0)


def _dispatch(dest_flat, xn, n_rows, tb):
    N, D = xn.shape
    init = jnp.zeros((n_rows, D), xn.dtype)
    kern = functools.partial(_dispatch_kernel, tb=tb)
    return pl.pallas_call(
        kern, grid=(N // tb,),
        in_specs=[
            pl.BlockSpec((tb * TOP_K,), lambda i: (i,), memory_space=pltpu.SMEM),
            pl.BlockSpec((tb, D), lambda i: (i, 0)),
            pl.BlockSpec(memory_space=pl.ANY),
        ],
        out_specs=pl.BlockSpec(memory_space=pl.ANY),
        out_shape=jax.ShapeDtypeStruct((n_rows, D), xn.dtype),
        scratch_shapes=[pltpu.SemaphoreType.DMA(())],
        input_output_aliases={2: 0},
        compiler_params=_params(1), name="moe_dispatch",
    )(dest_flat, xn, init)


def _expert_kernel(bexp_ref, nused_ref, x_ref, wu_ref, bu_ref, wd_ref, bd_ref, y_ref):
    del bexp_ref
    i = pl.program_id(0)

    @pl.when(i < nused_ref[0])
    def _():
        de = wd_ref.shape[0]
        h = jnp.dot(x_ref[...].astype(BF16), wu_ref[...], preferred_element_type=F32) + bu_ref[...]
        gt = jnp.minimum(h[:, :de], SWIGLU_LIMIT)
        lin = jnp.clip(h[:, de:], -SWIGLU_LIMIT, SWIGLU_LIMIT)
        act = gt * jax.nn.sigmoid(SWIGLU_ALPHA * gt) * (lin + 1.0)
        y_ref[...] = jnp.dot(act.astype(BF16), wd_ref[...], preferred_element_type=F32) + bd_ref[...]

    @pl.when(i >= nused_ref[0])
    def _():
        y_ref[...] = jnp.zeros(y_ref.shape, F32)


def _experts(block_exp, n_used, xs, w_up, b_up, w_down, b_down, tmx):
    P, D = xs.shape
    E, _, two_de = w_up.shape
    de = two_de // 2
    blk = lambda i, be, nu: (jnp.minimum(i, nu[0] - 1), 0)
    grid_spec = pltpu.PrefetchScalarGridSpec(
        num_scalar_prefetch=2, grid=(P // tmx,),
        in_specs=[
            pl.BlockSpec((tmx, D), blk),
            pl.BlockSpec((None, D, two_de), lambda i, be, nu: (be[i], 0, 0)),
            pl.BlockSpec((None, 1, two_de), lambda i, be, nu: (be[i], 0, 0)),
            pl.BlockSpec((None, de, D), lambda i, be, nu: (be[i], 0, 0)),
            pl.BlockSpec((None, 1, D), lambda i, be, nu: (be[i], 0, 0)),
        ],
        out_specs=pl.BlockSpec((tmx, D), lambda i, be, nu: (i, 0)),
    )
    return pl.pallas_call(
        _expert_kernel, grid_spec=grid_spec,
        out_shape=jax.ShapeDtypeStruct((P, D), F32),
        compiler_params=_params(1), name="moe_experts",
    )(block_exp, n_used, xs, w_up, b_up.reshape(E, 1, two_de), w_down, b_down.reshape(E, 1, D))


def _combine_kernel(dest_ref, h1_ref, gate_ref, p_ref, ys_ref, gple_ref, wg_ref, wp_ref, gfin_ref,
                    o_ref, gbuf, sem, *, tb):
    def start(r, c):
        for kk in range(TOP_K):
            _row_copy(ys_ref, dest_ref[r * TOP_K + kk], gbuf.at[kk], r, sem).start()
        return c

    lax.fori_loop(0, tb, start, 0)

    def wait(r, c):
        for kk in range(TOP_K):
            _row_copy(ys_ref, 0, gbuf.at[kk], 0, sem).wait()
        return c

    lax.fori_loop(0, tb, wait, 0)

    g = gate_ref[...]
    h2 = h1_ref[...]
    for kk in range(TOP_K):
        h2 = h2 + g[:, kk:kk + 1] * gbuf[kk]
    z = jnp.dot(_rms(h2, gple_ref[...]).astype(BF16), wg_ref[...], preferred_element_type=F32)
    pe = jnp.dot(p_ref[...].astype(BF16), wp_ref[...], preferred_element_type=F32)
    h3 = h2 + pe * jax.nn.sigmoid(z)
    o_ref[...] = _rms(h3, gfin_ref[...])


def _combine(dest_flat, h1, gates, p2, ys, g_ple, w_gate, w_proj, g_final, tb):
    N, D = h1.shape
    pd = p2.shape[1]
    kern = functools.partial(_combine_kernel, tb=tb)
    full = lambda shape: pl.BlockSpec(shape, lambda i: (0,) * len(shape))
    return pl.pallas_call(
        kern, grid=(N // tb,),
        in_specs=[
            pl.BlockSpec((tb * TOP_K,), lambda i: (i,), memory_space=pltpu.SMEM),
            pl.BlockSpec((tb, D), lambda i: (i, 0)),
            pl.BlockSpec((tb, TOP_K), lambda i: (i, 0)),
            pl.BlockSpec((tb, pd), lambda i: (i, 0)),
            pl.BlockSpec(memory_space=pl.ANY),
            full((1, D)), full((D, D)), full((pd, D)), full((1, D)),
        ],
        out_specs=pl.BlockSpec((tb, D), lambda i: (i, 0)),
        out_shape=jax.ShapeDtypeStruct((N, D), F32),
        scratch_shapes=[pltpu.VMEM((TOP_K, tb, D), F32), pltpu.SemaphoreType.DMA(())],
        compiler_params=_params(1), name="combine_ple_norm",
    )(dest_flat, h1, gates, p2, ys, g_ple, w_gate, w_proj, g_final)


def _pick(n, prefs):
    for t in prefs:
        if n % t == 0:
            return t
    return n


def _rope_tables(S):
    inv = ROPE_THETA ** (-jnp.arange(0, HEAD_DIM, 2, dtype=F32) / HEAD_DIM)
    ang = jnp.arange(S, dtype=F32)[:, None] * inv[None, :]
    cos = jnp.cos(ang)
    sin = jnp.sin(ang)
    reps = LANES // HEAD_DIM
    return jnp.tile(jnp.concatenate([cos, cos], -1), (1, reps)), jnp.tile(jnp.concatenate([sin, sin], -1), (1, reps))


def _rot_cols(w, n_heads):
    D = w.shape[0]
    w3 = w.reshape(D, n_heads, 2, HEAD_DIM // 2)
    return jnp.stack([-w3[:, :, 1], w3[:, :, 0]], axis=2).reshape(D, n_heads * HEAD_DIM)


def _layer(h, p, w_in, w_out, g_mix, g_ffn, glu_b, dw_w, dw_b, ln_g, ln_b, w_router, b_router,
           w_up, b_up, w_down, b_down, w_ple_proj, w_ple_gate, g_ple, g_final):
    B, S, D = h.shape
    N = B * S
    aw = ATTN_HEADS * HEAD_DIM
    iw = IDX_HEADS * IDX_DIM
    o = np.cumsum([0, aw, HEAD_DIM, HEAD_DIM, iw, IDX_DIM, IDX_HEADS])
    wq, wk, wv, wqi, wki, wwi, wcu = (w_in[:, o[0]:o[1]], w_in[:, o[1]:o[2]], w_in[:, o[2]:o[3]],
                                      w_in[:, o[3]:o[4]], w_in[:, o[4]:o[5]], w_in[:, o[5]:o[6]], w_in[:, o[6]:])
    rope_w = jnp.concatenate([wq, wqi, wk, wki], axis=1)
    rope_rot = jnp.concatenate([_rot_cols(wq, ATTN_HEADS), _rot_cols(wqi, IDX_HEADS),
                                _rot_cols(wk, 1), _rot_cols(wki, 1)], axis=1)
    misc = jnp.concatenate([wwi, jnp.zeros((D, LANES - IDX_HEADS - HEAD_DIM), F32), wv], axis=1)
    w_all = jnp.concatenate([rope_w, rope_rot, misc, wcu], axis=1).astype(BF16)
    cos_t, sin_t = _rope_tables(S)

    tm = _pick(S, (512, 256, 128))
    q, qi, k, ki, v, wi, conv = _inproj(h, g_mix[None], w_all, cos_t, sin_t, glu_b[None], dw_w, dw_b[None],
                                        ln_g[None], ln_b[None], tm)
    qb = _pick(S, (128,))
    sc = _pick(S, (512, 256, 128))
    attn = _dsa(q, qi, wi, k, ki, v, qb, sc)

    h1, xn, route, gates, cnt = _mix_router(h.reshape(N, D), attn, conv, w_out.astype(BF16), g_ffn[None],
                                            w_router.astype(BF16), b_router[None], tm)

    tmx = 512
    counts = cnt[0, :N_EXPERTS].astype(I32)
    padded = (counts + tmx - 1) // tmx * tmx
    cum = jnp.cumsum(padded)
    pstart = cum - padded
    n_blocks = (N * TOP_K + N_EXPERTS * (tmx - 1) + tmx - 1) // tmx
    dest = (pstart[route[:, :TOP_K]] + route[:, TOP_K:]).reshape(N * TOP_K)
    n_used = (cum[-1] // tmx).astype(I32)
    blk_start = jnp.minimum(jnp.arange(n_blocks, dtype=I32), n_used - 1) * tmx
    block_exp = jnp.minimum(jnp.searchsorted(cum, blk_start, side='right'), N_EXPERTS - 1).astype(I32)

    tb = _pick(N, (256,))
    xs = _dispatch(dest, xn, n_blocks * tmx, tb)
    ys = _experts(block_exp, n_used.reshape(1), xs, w_up.astype(BF16), b_up, w_down.astype(BF16), b_down, tmx)
    out = _combine(dest, h1, gates, p.reshape(N, -1), ys, g_ple[None], w_ple_gate.astype(BF16),
                   w_ple_proj.astype(BF16), g_final[None], tb)
    return out.reshape(B, S, D)


def kernel(x, p, w_in, w_out, g_mix, g_ffn, conv_glu_b, conv_dw_w, conv_dw_b, conv_ln_g, conv_ln_b, w_router,
           b_router, w_up, b_up, w_down, b_down, w_ple_proj, w_ple_gate, g_ple, g_final):
    assert p.shape[0] == 1, "single trunk layer"
    return _layer(x, p[0], w_in[0], w_out[0], g_mix[0], g_ffn[0], conv_glu_b[0], conv_dw_w[0], conv_dw_b[0],
                  conv_ln_g[0], conv_ln_b[0], w_router[0], b_router[0], w_up[0], b_up[0], w_down[0], b_down[0],
                  w_ple_proj[0], w_ple_gate[0], g_ple[0], g_final)
```

```python
import functools

import jax
import jax.numpy as jnp
import numpy as np
from jax import lax
from jax.experimental import pallas as pl
from jax.experimental.pallas import tpu as pltpu

F32 = jnp.float32
BF16 = jnp.bfloat16
I32 = jnp.int32

ATTN_HEADS = 8
HEAD_DIM = 64
IDX_HEADS = 4
IDX_DIM = 64
TOPK_MAX = 256
CONV_KERNEL = 31
N_EXPERTS = 32
TOP_K = 4
SWIGLU_LIMIT = 7.0
SWIGLU_ALPHA = 1.702
ROPE_THETA = 10000.0
EPS = 1e-6

LANES = 128
SUBLANES = 8
CONV_HALO = 32
NEG_BIG = -1e30
INT_MIN = -2147483648
COUNT_CHAINS = 8
VMEM_LIMIT = 56 * 1024 * 1024


def _params(n_axes):
    return pltpu.CompilerParams(dimension_semantics=("arbitrary",) * n_axes,
                                vmem_limit_bytes=VMEM_LIMIT)


def _rms(x, g):
    return x * lax.rsqrt(jnp.mean(x * x, axis=-1, keepdims=True) + EPS) * g


def _dot_t(a, b):
    return lax.dot_general(a, b, (((1,), (1,)), ((), ())), preferred_element_type=F32)


def _inproj_kernel(x_ref, g_ref, w_ref, wwi_ref, cos_ref, sin_ref, glub_ref, dww_ref, dwb_ref, lng_ref, lnb_ref,
                   q_ref, qi_ref, k_ref, ki_ref, v_ref, wit_ref, conv_ref, hbuf, *, tm, conv_w, row_blk):
    t = pl.program_id(1)
    xn = _rms(x_ref[...], g_ref[...]).astype(BF16)
    u = jnp.dot(xn, w_ref[...], preferred_element_type=F32)
    rope_w = (ATTN_HEADS + IDX_HEADS + 2) * HEAD_DIM
    cos = cos_ref[...]
    sin = sin_ref[...]
    r = [u[:, c * LANES:(c + 1) * LANES] * cos + u[:, rope_w + c * LANES:rope_w + (c + 1) * LANES] * sin
         for c in range(rope_w // LANES)]
    for h in range(ATTN_HEADS):
        blk = r[h // 2][:, (h % 2) * HEAD_DIM:(h % 2 + 1) * HEAD_DIM]
        q_ref[h] = (blk * (HEAD_DIM ** -0.5)).astype(BF16)
    for h in range(IDX_HEADS):
        blk = r[ATTN_HEADS // 2 + h // 2][:, (h % 2) * IDX_DIM:(h % 2 + 1) * IDX_DIM]
        qi_ref[h] = blk.astype(BF16)
    last = r[(ATTN_HEADS + IDX_HEADS) // 2]
    k_ref[...] = last[:, :HEAD_DIM].astype(BF16)
    ki_ref[...] = last[:, HEAD_DIM:].astype(BF16)
    v_ref[...] = u[:, 2 * rope_w + LANES - HEAD_DIM:2 * rope_w + LANES].astype(BF16)
    wit_ref[...] = _dot_t(wwi_ref[...], xn)

    cu = u[:, 2 * rope_w + LANES:] + glub_ref[...]
    hh = cu[:, :conv_w] * jax.nn.sigmoid(cu[:, conv_w:])

    @pl.when(t == 0)
    def _():
        hbuf[0:CONV_HALO, :] = jnp.zeros((CONV_HALO, conv_w), F32)

    hbuf[CONV_HALO:CONV_HALO + tm, :] = hh
    base = CONV_HALO - (CONV_KERNEL - 1)
    for rb in range(tm // row_blk):
        acc = jnp.zeros((row_blk, conv_w), F32)
        for j in range(CONV_KERNEL):
            s0 = rb * row_blk + base + j
            acc = acc + hbuf[s0:s0 + row_blk, :] * dww_ref[j:j + 1, :]
        y = acc + dwb_ref[...]
        mu = jnp.mean(y, axis=-1, keepdims=True)
        yc = y - mu
        var = jnp.mean(yc * yc, axis=-1, keepdims=True)
        z = yc * lax.rsqrt(var + EPS) * lng_ref[...] + lnb_ref[...]
        conv_ref[rb * row_blk:(rb + 1) * row_blk, :] = (z * jax.nn.sigmoid(z)).astype(BF16)
    hbuf[0:CONV_HALO, :] = hbuf[tm:tm + CONV_HALO, :]


def _inproj(x, g_mix, w_all, w_wi_t, cos_t, sin_t, glu_b, dw_w, dw_b, ln_g, ln_b, tm):
    B, S, D = x.shape
    conv_w = dw_w.shape[1]
    nt = S // tm
    wtot = w_all.shape[1]
    row_blk = 64 if tm % 64 == 0 else tm
    kern = functools.partial(_inproj_kernel, tm=tm, conv_w=conv_w, row_blk=row_blk)
    full = lambda shape: pl.BlockSpec(shape, lambda b, t: (0,) * len(shape))
    out_shape = (
        jax.ShapeDtypeStruct((B, ATTN_HEADS, S, HEAD_DIM), BF16),
        jax.ShapeDtypeStruct((B, IDX_HEADS, S, IDX_DIM), BF16),
        jax.ShapeDtypeStruct((B, S, HEAD_DIM), BF16),
        jax.ShapeDtypeStruct((B, S, IDX_DIM), BF16),
        jax.ShapeDtypeStruct((B, S, HEAD_DIM), BF16),
        jax.ShapeDtypeStruct((B, SUBLANES, S), F32),
        jax.ShapeDtypeStruct((B, S, conv_w), BF16),
    )
    out_specs = (
        pl.BlockSpec((None, ATTN_HEADS, tm, HEAD_DIM), lambda b, t: (b, 0, t, 0)),
        pl.BlockSpec((None, IDX_HEADS, tm, IDX_DIM), lambda b, t: (b, 0, t, 0)),
        pl.BlockSpec((None, tm, HEAD_DIM), lambda b, t: (b, t, 0)),
        pl.BlockSpec((None, tm, IDX_DIM), lambda b, t: (b, t, 0)),
        pl.BlockSpec((None, tm, HEAD_DIM), lambda b, t: (b, t, 0)),
        pl.BlockSpec((None, SUBLANES, tm), lambda b, t: (b, 0, t)),
        pl.BlockSpec((None, tm, conv_w), lambda b, t: (b, t, 0)),
    )
    in_specs = [
        pl.BlockSpec((None, tm, D), lambda b, t: (b, t, 0)),
        full((1, D)),
        full((D, wtot)),
        full((SUBLANES, D)),
        pl.BlockSpec((tm, LANES), lambda b, t: (t, 0)),
        pl.BlockSpec((tm, LANES), lambda b, t: (t, 0)),
        full((1, 2 * conv_w)),
        full((CONV_KERNEL, conv_w)),
        full((1, conv_w)),
        full((1, conv_w)),
        full((1, conv_w)),
    ]
    return pl.pallas_call(
        kern, grid=(B, nt), in_specs=in_specs, out_specs=out_specs, out_shape=out_shape,
        scratch_shapes=[pltpu.VMEM((CONV_HALO + tm, conv_w), F32)],
        compiler_params=_params(2), name="inproj_rope_conv",
    )(x, g_mix, w_all, w_wi_t, cos_t, sin_t, glu_b, dw_w, dw_b, ln_g, ln_b)


def _count(pred, key_scr, nk, qb, sc):
    lanes_acc = COUNT_CHAINS * SUBLANES

    def chunk_body(c, acc):
        off = pl.multiple_of(c * sc, sc)
        keys3 = key_scr[pl.ds(off, sc), :].reshape(sc // SUBLANES, SUBLANES, qb)
        ones = jnp.where(pred(keys3, off), 1.0, 0.0).reshape(sc // lanes_acc, lanes_acc, qb)
        return acc + jnp.sum(ones, axis=0)
    acc = lax.fori_loop(0, nk, chunk_body, jnp.zeros((lanes_acc, qb), F32))
    acc = jnp.sum(acc.reshape(COUNT_CHAINS, SUBLANES, qb), axis=0)
    return jnp.broadcast_to(jnp.sum(acc, axis=0, keepdims=True), (SUBLANES, qb))


def _dsa_kernel(q_ref, qi_ref, wit_ref, k_ref, ki_ref, v_ref, o_ref, key_scr, m_scr, l_scr, acc_scr,
                *, qb, sc, n_sel, idx_bits, seq):
    j = pl.program_id(1)
    q_lo = j * qb
    nk = (q_lo + qb + sc - 1) // sc
    n_sel_f = jnp.float32(n_sel)
    q_pos = q_lo + lax.broadcasted_iota(I32, (sc, qb), 1)
    k_in_chunk = lax.broadcasted_iota(I32, (sc, qb), 0)
    k_in_chunk3 = k_in_chunk.reshape(sc // SUBLANES, SUBLANES, qb)

    w_eff = wit_ref[...] * (IDX_DIM ** -0.5 * IDX_HEADS ** -0.5)

    def score_body(c, carry):
        off = pl.multiple_of(c * sc, sc)
        kc = ki_ref[pl.ds(off, sc), :]
        sco = jnp.zeros((sc, qb), F32)
        for h in range(IDX_HEADS):
            sco = sco + jnp.maximum(_dot_t(kc, qi_ref[h]), 0.0) * w_eff[h:h + 1, :]
        sco = jnp.where(sco == 0.0, 0.0, sco)
        sco = jnp.where(off + k_in_chunk <= q_pos, sco, -jnp.inf)
        bits = pltpu.bitcast(sco, I32)
        key_scr[pl.ds(off, sc), :] = jnp.where(bits < 0, bits ^ jnp.int32(0x7FFFFFFF), bits)
        return carry

    lax.fori_loop(0, nk, score_body, 0)

    cnt0 = _count(lambda k3, off: k3 >= 0, key_scr, nk, qb, sc)
    thr = jnp.where(cnt0 >= n_sel_f, jnp.int32(0), jnp.int32(INT_MIN))

    def bit_body(i, thr):
        cand = thr | lax.shift_left(jnp.int32(1), jnp.int32(30) - i)
        cnt = _count(lambda k3, off: k3 >= cand[None], key_scr, nk, qb, sc)
        return jnp.where(cnt >= n_sel_f, cand, thr)

    thr = lax.fori_loop(0, 31, bit_body, thr)

    cnt_ge = _count(lambda k3, off: k3 >= thr[None], key_scr, nk, qb, sc)

    def tie_break():
        cnt_gt = _count(lambda k3, off: k3 > thr[None], key_scr, nk, qb, sc)
        need = n_sel_f - cnt_gt

        def idx_body(i, jj):
            cand = jj | lax.shift_left(jnp.int32(1), jnp.int32(idx_bits - 1) - i)
            cnt = _count(lambda k3, off: jnp.logical_and(k3 == thr[None], off + k_in_chunk3 < cand[None]),
                         key_scr, nk, qb, sc)
            return jnp.where(cnt < need, cand, jj)

        return lax.fori_loop(0, idx_bits, idx_body, jnp.zeros((SUBLANES, qb), I32))

    jmax = lax.cond(jnp.max(cnt_ge) > n_sel_f, tie_break, lambda: jnp.full((SUBLANES, qb), seq, I32))

    nh = ATTN_HEADS
    q2 = q_ref[...].reshape(nh * qb, HEAD_DIM)
    m_scr[...] = jnp.full((nh * qb, LANES), NEG_BIG, F32)
    l_scr[...] = jnp.zeros((nh * qb, LANES), F32)
    acc_scr[...] = jnp.zeros((nh * qb, HEAD_DIM), F32)
    thr_row = thr[0:1, :]
    jmax_row = jmax[0:1, :]

    def attn_body(c, carry):
        off = pl.multiple_of(c * sc, sc)
        kc = k_ref[pl.ds(off, sc), :]
        vc = v_ref[pl.ds(off, sc), :]
        keys = key_scr[pl.ds(off, sc), :]
        kidx = off + k_in_chunk
        tie = jnp.logical_and(keys == thr_row, kidx <= jmax_row)
        sel = jnp.logical_and(jnp.logical_or(keys > thr_row, tie), kidx <= q_pos)
        bias = jnp.where(sel, 0.0, NEG_BIG).T
        s = _dot_t(q2, kc)
        s = (s.reshape(nh, qb, sc) + bias[None]).reshape(nh * qb, sc)
        m_old = m_scr[...]
        m_new = jnp.maximum(m_old, jnp.broadcast_to(jnp.max(s, axis=-1, keepdims=True), m_old.shape))
        alpha = jnp.exp(m_old - m_new)
        p = jnp.exp(s - jnp.tile(m_new, (1, sc // LANES)))
        l_scr[...] = alpha * l_scr[...] + jnp.broadcast_to(jnp.sum(p, axis=-1, keepdims=True), m_old.shape)
        pv = jnp.dot(p.astype(BF16), vc, preferred_element_type=F32)
        acc_scr[...] = alpha[:, :HEAD_DIM] * acc_scr[...] + pv
        m_scr[...] = m_new
        return carry

    lax.fori_loop(0, nk, attn_body, 0)
    out = acc_scr[...] / l_scr[...][:, :HEAD_DIM]
    o_ref[...] = out.reshape(nh, qb, HEAD_DIM).astype(BF16)


def _dsa(q, qi, wit, k, ki, v, qb, sc):
    B, H, S, dh = q.shape
    n_sel = min(TOPK_MAX, S // 4)
    idx_bits = max(1, int(np.ceil(np.log2(S))))
    assert S % sc == 0 and sc % qb == 0 and qb % LANES == 0
    kern = functools.partial(_dsa_kernel, qb=qb, sc=sc, n_sel=n_sel, idx_bits=idx_bits, seq=S)
    in_specs = [
        pl.BlockSpec((None, H, qb, dh), lambda b, j: (b, 0, j, 0)),
        pl.BlockSpec((None, IDX_HEADS, qb, IDX_DIM), lambda b, j: (b, 0, j, 0)),
        pl.BlockSpec((None, SUBLANES, qb), lambda b, j: (b, 0, j)),
        pl.BlockSpec((None, S, dh), lambda b, j: (b, 0, 0)),
        pl.BlockSpec((None, S, IDX_DIM), lambda b, j: (b, 0, 0)),
        pl.BlockSpec((None, S, dh), lambda b, j: (b, 0, 0)),
    ]
    return pl.pallas_call(
        kern, grid=(B, S // qb), in_specs=in_specs,
        out_specs=pl.BlockSpec((None, H, qb, dh), lambda b, j: (b, 0, j, 0)),
        out_shape=jax.ShapeDtypeStruct((B, H, S, dh), BF16),
        scratch_shapes=[
            pltpu.VMEM((S, qb), I32),
            pltpu.VMEM((H * qb, LANES), F32),
            pltpu.VMEM((H * qb, LANES), F32),
            pltpu.VMEM((H * qb, dh), F32),
        ],
        compiler_params=_params(2), name="dsa_attention",
    )(q, qi, wit, k, ki, v)


def _mix_router_kernel(x_ref, attn_ref, conv_ref, wout_ref, g_ref, wr_ref, br_ref,
                       h1_ref, xn_ref, route_ref, gate_ref, cnt_ref, carry, *, tm):
    i = pl.program_id(0)

    @pl.when(i == 0)
    def _():
        carry[...] = jnp.zeros(carry.shape, F32)

    attn_w = ATTN_HEADS * HEAD_DIM
    acc = x_ref[...] + jnp.dot(conv_ref[...], wout_ref[attn_w:, :], preferred_element_type=F32)
    for h in range(ATTN_HEADS):
        acc = acc + jnp.dot(attn_ref[h], wout_ref[h * HEAD_DIM:(h + 1) * HEAD_DIM, :],
                            preferred_element_type=F32)
    h1_ref[...] = acc
    xn = _rms(acc, g_ref[...])
    xn_ref[...] = xn
    logits = jnp.dot(xn.astype(BF16), wr_ref[...], preferred_element_type=F32) + br_ref[...]

    ne = logits.shape[1]
    eid = lax.broadcasted_iota(I32, (tm, ne), 1)
    work = logits
    vals, hots = [], []
    for _ in range(TOP_K):
        m = jnp.max(work, axis=-1, keepdims=True)
        e = jnp.min(jnp.where(work == m, eid, ne), axis=-1, keepdims=True)
        hot = eid == e
        work = jnp.where(hot, -jnp.inf, work)
        vals.append(m)
        hots.append(hot)
    ex = [jnp.exp(v - vals[0]) for v in vals]
    den = ex[0] + ex[1] + ex[2] + ex[3]

    multi = jnp.zeros((tm, ne), F32)
    for hot in hots:
        multi = multi + jnp.where(hot, 1.0, 0.0)
    ri = lax.broadcasted_iota(I32, (tm, tm), 0)
    ci = lax.broadcasted_iota(I32, (tm, tm), 1)
    tri = jnp.where(ci < ri, 1.0, 0.0).astype(BF16)
    before = jnp.dot(tri, multi.astype(BF16), preferred_element_type=F32) + carry[0:1, :ne]
    carry[0:1, :ne] = carry[0:1, :ne] + jnp.sum(multi, axis=0, keepdims=True)
    cnt_ref[...] = carry[...]

    lane8 = lax.broadcasted_iota(I32, (tm, 2 * TOP_K), 1)
    lane4 = lax.broadcasted_iota(I32, (tm, TOP_K), 1)
    route = jnp.zeros((tm, 2 * TOP_K), I32)
    gates = jnp.zeros((tm, TOP_K), F32)
    for kk in range(TOP_K):
        e = jnp.sum(jnp.where(hots[kk], eid, 0), axis=-1, keepdims=True)
        rk = jnp.sum(jnp.where(hots[kk], before, 0.0), axis=-1, keepdims=True).astype(I32)
        route = route + jnp.where(lane8 == kk, e, 0) + jnp.where(lane8 == TOP_K + kk, rk, 0)
        gates = gates + jnp.where(lane4 == kk, ex[kk] / den, 0.0)
    route_ref[...] = route
    gate_ref[...] = gates


def _mix_router(x2, attn, conv, w_out, g_ffn, w_router, b_router, tm):
    N, D = x2.shape
    B, H, S, dh = attn.shape
    nt = S // tm
    ne = w_router.shape[1]
    cw = conv.shape[-1]
    kern = functools.partial(_mix_router_kernel, tm=tm)
    full = lambda shape: pl.BlockSpec(shape, lambda i: (0,) * len(shape))
    in_specs = [
        pl.BlockSpec((tm, D), lambda i: (i, 0)),
        pl.BlockSpec((None, H, tm, dh), lambda i: (i // nt, 0, i % nt, 0)),
        pl.BlockSpec((None, tm, cw), lambda i: (i // nt, i % nt, 0)),
        full(w_out.shape), full((1, D)), full((D, ne)), full((1, ne)),
    ]
    out_shape = (
        jax.ShapeDtypeStruct((N, D), F32),
        jax.ShapeDtypeStruct((N, D), F32),
        jax.ShapeDtypeStruct((N, 2 * TOP_K), I32),
        jax.ShapeDtypeStruct((N, TOP_K), F32),
        jax.ShapeDtypeStruct((SUBLANES, LANES), F32),
    )
    out_specs = (
        pl.BlockSpec((tm, D), lambda i: (i, 0)),
        pl.BlockSpec((tm, D), lambda i: (i, 0)),
        pl.BlockSpec((tm, 2 * TOP_K), lambda i: (i, 0)),
        pl.BlockSpec((tm, TOP_K), lambda i: (i, 0)),
        pl.BlockSpec((SUBLANES, LANES), lambda i: (0, 0)),
    )
    return pl.pallas_call(
        kern, grid=(N // tm,), in_specs=in_specs, out_specs=out_specs, out_shape=out_shape,
        scratch_shapes=[pltpu.VMEM((SUBLANES, LANES), F32)],
        compiler_params=_params(1), name="outproj_router",
    )(x2, attn, conv, w_out, g_ffn, w_router, b_router)


def _row_copy(src_ref, src_row, dst_ref, dst_row, sem):
    return pltpu.make_async_copy(src_ref.at[pl.ds(src_row, 1)], dst_ref.at[pl.ds(dst_row, 1)], sem)


def _dispatch_kernel(zstart_ref, dest_ref, x_ref, out_ref, zbuf, sem, zsem, *, tb, tmx, n_blocks):
    zspan = zbuf.shape[0]

    @pl.when(pl.program_id(0) == 0)
    def _():
        zbuf[...] = jnp.zeros(zbuf.shape, zbuf.dtype)
        for e in range(N_EXPERTS):
            z0 = pl.multiple_of(zstart_ref[e], SUBLANES)
            pltpu.make_async_copy(zbuf, out_ref.at[pl.ds(z0, zspan)], zsem).start()
        for e in range(N_EXPERTS):
            pltpu.make_async_copy(zbuf, out_ref.at[pl.ds(0, zspan)], zsem).wait()

        def tail_copy(i):
            return pltpu.make_async_copy(zbuf.at[pl.ds(0, tmx)], out_ref.at[pl.ds(pl.multiple_of(i * tmx, tmx), tmx)],
                                         zsem)

        def tail_start(i, c):
            tail_copy(i).start()
            return c

        def tail_wait(i, c):
            tail_copy(i).wait()
            return c

        lax.fori_loop(zstart_ref[N_EXPERTS], n_blocks, tail_start, 0)
        lax.fori_loop(zstart_ref[N_EXPERTS], n_blocks, tail_wait, 0)

    def start(r, c):
        for kk in range(TOP_K):
            _row_copy(x_ref, r, out_ref, dest_ref[r * TOP_K + kk], sem).start()
        return c

    lax.fori_loop(0, tb, start, 0)

    def wait(r, c):
        for kk in range(TOP_K):
            _row_copy(x_ref, 0, out_ref, 0, sem).wait()
        return c

    lax.fori_loop(0, tb, wait, 0)


def _dispatch(zstart, dest_flat, xn, n_rows, tb, tmx, zspan):
    N, D = xn.shape
    kern = functools.partial(_dispatch_kernel, tb=tb, tmx=tmx, n_blocks=n_rows // tmx)
    grid_spec = pltpu.PrefetchScalarGridSpec(
        num_scalar_prefetch=1, grid=(N // tb,),
        in_specs=[
            pl.BlockSpec((tb * TOP_K,), lambda i, zs: (i,), memory_space=pltpu.SMEM),
            pl.BlockSpec((tb, D), lambda i, zs: (i, 0)),
        ],
        out_specs=pl.BlockSpec(memory_space=pl.ANY),
        scratch_shapes=[pltpu.VMEM((zspan, D), xn.dtype), pltpu.SemaphoreType.DMA(()),
                        pltpu.SemaphoreType.DMA(())],
    )
    return pl.pallas_call(
        kern, grid_spec=grid_spec,
        out_shape=jax.ShapeDtypeStruct((n_rows, D), xn.dtype),
        compiler_params=_params(1), name="moe_dispatch",
    )(zstart, dest_flat, xn)


def _expert_kernel(bexp_ref, nused_ref, x_ref, wu_ref, bu_ref, wd_ref, bd_ref, y_ref):
    del bexp_ref
    i = pl.program_id(0)

    @pl.when(i < nused_ref[0])
    def _():
        de = wd_ref.shape[0]
        h = jnp.dot(x_ref[...].astype(BF16), wu_ref[...], preferred_element_type=F32) + bu_ref[...]
        gt = jnp.minimum(h[:, :de], SWIGLU_LIMIT)
        lin = jnp.clip(h[:, de:], -SWIGLU_LIMIT, SWIGLU_LIMIT)
        act = gt * jax.nn.sigmoid(SWIGLU_ALPHA * gt) * (lin + 1.0)
        y_ref[...] = jnp.dot(act.astype(BF16), wd_ref[...], preferred_element_type=F32) + bd_ref[...]

    @pl.when(i >= nused_ref[0])
    def _():
        y_ref[...] = jnp.zeros(y_ref.shape, F32)


def _experts(block_exp, n_used, xs, w_up, b_up, w_down, b_down, tmx):
    P, D = xs.shape
    E, _, two_de = w_up.shape
    de = two_de // 2
    grid_spec = pltpu.PrefetchScalarGridSpec(
        num_scalar_prefetch=2, grid=(P // tmx,),
        in_specs=[
            pl.BlockSpec((tmx, D), lambda i, be, nu: (jnp.minimum(i, nu[0] - 1), 0)),
            pl.BlockSpec((None, D, two_de), lambda i, be, nu: (be[i], 0, 0)),
            pl.BlockSpec((None, 1, two_de), lambda i, be, nu: (be[i], 0, 0)),
            pl.BlockSpec((None, de, D), lambda i, be, nu: (be[i], 0, 0)),
            pl.BlockSpec((None, 1, D), lambda i, be, nu: (be[i], 0, 0)),
        ],
        out_specs=pl.BlockSpec((tmx, D), lambda i, be, nu: (i, 0)),
    )
    return pl.pallas_call(
        _expert_kernel, grid_spec=grid_spec,
        out_shape=jax.ShapeDtypeStruct((P, D), F32),
        compiler_params=_params(1), name="moe_experts",
    )(block_exp, n_used, xs, w_up, b_up.reshape(E, 1, two_de), w_down, b_down.reshape(E, 1, D))


def _combine_kernel(dest_ref, h1_ref, gate_ref, p_ref, ys_ref, gple_ref, wg_ref, wp_ref, gfin_ref,
                    o_ref, gbuf, sem, *, tb):
    def start(r, c):
        for kk in range(TOP_K):
            _row_copy(ys_ref, dest_ref[r * TOP_K + kk], gbuf.at[kk], r, sem).start()
        return c

    lax.fori_loop(0, tb, start, 0)

    def wait(r, c):
        for kk in range(TOP_K):
            _row_copy(ys_ref, 0, gbuf.at[kk], 0, sem).wait()
        return c

    lax.fori_loop(0, tb, wait, 0)

    g = gate_ref[...]
    h2 = h1_ref[...]
    for kk in range(TOP_K):
        h2 = h2 + g[:, kk:kk + 1] * gbuf[kk]
    z = jnp.dot(_rms(h2, gple_ref[...]).astype(BF16), wg_ref[...], preferred_element_type=F32)
    pe = jnp.dot(p_ref[...].astype(BF16), wp_ref[...], preferred_element_type=F32)
    h3 = h2 + pe * jax.nn.sigmoid(z)
    o_ref[...] = _rms(h3, gfin_ref[...])


def _combine(dest_flat, h1, gates, p2, ys, g_ple, w_gate, w_proj, g_final, tb):
    N, D = h1.shape
    pd = p2.shape[1]
    kern = functools.partial(_combine_kernel, tb=tb)
    full = lambda shape: pl.BlockSpec(shape, lambda i: (0,) * len(shape))
    return pl.pallas_call(
        kern, grid=(N // tb,),
        in_specs=[
            pl.BlockSpec((tb * TOP_K,), lambda i: (i,), memory_space=pltpu.SMEM),
            pl.BlockSpec((tb, D), lambda i: (i, 0)),
            pl.BlockSpec((tb, TOP_K), lambda i: (i, 0)),
            pl.BlockSpec((tb, pd), lambda i: (i, 0)),
            pl.BlockSpec(memory_space=pl.ANY),
            full((1, D)), full((D, D)), full((pd, D)), full((1, D)),
        ],
        out_specs=pl.BlockSpec((tb, D), lambda i: (i, 0)),
        out_shape=jax.ShapeDtypeStruct((N, D), F32),
        scratch_shapes=[pltpu.VMEM((TOP_K, tb, D), F32), pltpu.SemaphoreType.DMA(())],
        compiler_params=_params(1), name="combine_ple_norm",
    )(dest_flat, h1, gates, p2, ys, g_ple, w_gate, w_proj, g_final)


def _pick(n, prefs):
    for t in prefs:
        if n % t == 0:
            return t
    return n


def _rope_tables(S):
    inv = ROPE_THETA ** (-jnp.arange(0, HEAD_DIM, 2, dtype=F32) / HEAD_DIM)
    ang = jnp.arange(S, dtype=F32)[:, None] * inv[None, :]
    cos = jnp.cos(ang)
    sin = jnp.sin(ang)
    reps = LANES // HEAD_DIM
    return jnp.tile(jnp.concatenate([cos, cos], -1), (1, reps)), jnp.tile(jnp.concatenate([sin, sin], -1), (1, reps))


def _rot_cols(w, n_heads):
    D = w.shape[0]
    w3 = w.reshape(D, n_heads, 2, HEAD_DIM // 2)
    return jnp.stack([-w3[:, :, 1], w3[:, :, 0]], axis=2).reshape(D, n_heads * HEAD_DIM)


def _layer(h, p, w_in, w_out, g_mix, g_ffn, glu_b, dw_w, dw_b, ln_g, ln_b, w_router, b_router,
           w_up, b_up, w_down, b_down, w_ple_proj, w_ple_gate, g_ple, g_final):
    B, S, D = h.shape
    N = B * S
    aw = ATTN_HEADS * HEAD_DIM
    iw = IDX_HEADS * IDX_DIM
    o = np.cumsum([0, aw, HEAD_DIM, HEAD_DIM, iw, IDX_DIM, IDX_HEADS])
    wq, wk, wv, wqi, wki, wwi, wcu = (w_in[:, o[0]:o[1]], w_in[:, o[1]:o[2]], w_in[:, o[2]:o[3]],
                                      w_in[:, o[3]:o[4]], w_in[:, o[4]:o[5]], w_in[:, o[5]:o[6]], w_in[:, o[6]:])
    rope_w = jnp.concatenate([wq, wqi, wk, wki], axis=1)
    rope_rot = jnp.concatenate([_rot_cols(wq, ATTN_HEADS), _rot_cols(wqi, IDX_HEADS),
                                _rot_cols(wk, 1), _rot_cols(wki, 1)], axis=1)
    misc = jnp.concatenate([jnp.zeros((D, LANES - HEAD_DIM), F32), wv], axis=1)
    w_all = jnp.concatenate([rope_w, rope_rot, misc, wcu], axis=1).astype(BF16)
    w_wi_t = jnp.concatenate([wwi.T, jnp.zeros((SUBLANES - IDX_HEADS, D), F32)], axis=0).astype(BF16)
    cos_t, sin_t = _rope_tables(S)

    tm = _pick(S, (512, 256, 128))
    q, qi, k, ki, v, wit, conv = _inproj(h, g_mix[None], w_all, w_wi_t, cos_t, sin_t, glu_b[None], dw_w,
                                         dw_b[None], ln_g[None], ln_b[None], tm)
    qb = _pick(S, (128,))
    sc = _pick(S, (512, 256, 128))
    attn = _dsa(q, qi, wit, k, ki, v, qb, sc)

    h1, xn, route, gates, cnt = _mix_router(h.reshape(N, D), attn, conv, w_out.astype(BF16), g_ffn[None],
                                            w_router.astype(BF16), b_router[None], tm)

    tmx = 512
    counts = cnt[0, :N_EXPERTS].astype(I32)
    padded = (counts + tmx - 1) // tmx * tmx
    cum = jnp.cumsum(padded)
    pstart = cum - padded
    n_blocks = (N * TOP_K + N_EXPERTS * (tmx - 1) + tmx - 1) // tmx
    n_rows = n_blocks * tmx
    dest = (pstart[route[:, :TOP_K]] + route[:, TOP_K:]).reshape(N * TOP_K)
    n_used = (cum[-1] // tmx).astype(I32)
    blk_start = jnp.minimum(jnp.arange(n_blocks, dtype=I32), n_used - 1) * tmx
    block_exp = jnp.minimum(jnp.sum((blk_start[:, None] >= cum[None, :]).astype(I32), axis=1), N_EXPERTS - 1)
    zspan = tmx + SUBLANES
    zstart = (jnp.minimum(pstart + counts, n_rows - zspan) // SUBLANES * SUBLANES).astype(I32)
    zstart = jnp.concatenate([zstart, n_used.reshape(1)])

    tb = _pick(N, (256,))
    xs = _dispatch(zstart, dest, xn, n_rows, tb, tmx, zspan)
    ys = _experts(block_exp, n_used.reshape(1), xs, w_up.astype(BF16), b_up, w_down.astype(BF16), b_down, tmx)
    out = _combine(dest, h1, gates, p.reshape(N, -1), ys, g_ple[None], w_ple_gate.astype(BF16),
                   w_ple_proj.astype(BF16), g_final[None], tb)
    return out.reshape(B, S, D)


def kernel(x, p, w_in, w_out, g_mix, g_ffn, conv_glu_b, conv_dw_w, conv_dw_b, conv_ln_g, conv_ln_b, w_router,
           b_router, w_up, b_up, w_down, b_down, w_ple_proj, w_ple_gate, g_ple, g_final):
    assert p.shape[0] == 1, "single trunk layer"
    return _layer(x, p[0], w_in[0], w_out[0], g_mix[0], g_ffn[0], conv_glu_b[0], conv_dw_w[0], conv_dw_b[0],
                  conv_ln_g[0], conv_ln_b[0], w_router[0], b_router[0], w_up[0], b_up[0], w_down[0], b_down[0],
                  w_ple_proj[0], w_ple_gate[0], g_ple[0], g_final)
```

```python
import functools

import jax
import jax.numpy as jnp
import numpy as np
from jax import lax
from jax.experimental import pallas as pl
from jax.experimental.pallas import tpu as pltpu

F32 = jnp.float32
BF16 = jnp.bfloat16
I32 = jnp.int32

ATTN_HEADS = 8
HEAD_DIM = 64
IDX_HEADS = 4
IDX_DIM = 64
TOPK_MAX = 256
CONV_KERNEL = 31
N_EXPERTS = 32
TOP_K = 4
SWIGLU_LIMIT = 7.0
SWIGLU_ALPHA = 1.702
ROPE_THETA = 10000.0
EPS = 1e-6

LANES = 128
SUBLANES = 8
CONV_HALO = 32
NEG_BIG = -1e30
INT_MIN = -2147483648
COUNT_CHAINS = 8
LOG2_E = 1.4426950408889634
HEADS_PER_GROUP = 8
VMEM_LIMIT = 56 * 1024 * 1024


def _params(n_axes):
    return pltpu.CompilerParams(dimension_semantics=("arbitrary",) * n_axes,
                                vmem_limit_bytes=VMEM_LIMIT)


def _rms(x, g):
    return x * lax.rsqrt(jnp.mean(x * x, axis=-1, keepdims=True) + EPS) * g


def _dot_t(a, b):
    return lax.dot_general(a, b, (((1,), (1,)), ((), ())), preferred_element_type=F32)


def _inproj_kernel(x_ref, g_ref, w_ref, wwi_ref, wvt_ref, cos_ref, sin_ref, glub_ref, dww_ref, dwb_ref, lng_ref,
                   lnb_ref, q_ref, qi_ref, k_ref, ki_ref, vt_ref, wit_ref, conv_ref, hbuf, *, tm, conv_w, row_blk):
    t = pl.program_id(1)
    xn = _rms(x_ref[...], g_ref[...]).astype(BF16)
    u = jnp.dot(xn, w_ref[...], preferred_element_type=F32)
    rope_w = (ATTN_HEADS + IDX_HEADS + 2) * HEAD_DIM
    cos = cos_ref[...]
    sin = sin_ref[...]
    r = [u[:, c * LANES:(c + 1) * LANES] * cos + u[:, rope_w + c * LANES:rope_w + (c + 1) * LANES] * sin
         for c in range(rope_w // LANES)]
    for h in range(ATTN_HEADS):
        blk = r[h // 2][:, (h % 2) * HEAD_DIM:(h % 2 + 1) * HEAD_DIM]
        q_ref[h] = (blk * (HEAD_DIM ** -0.5 * LOG2_E)).astype(BF16)
    for h in range(IDX_HEADS):
        blk = r[ATTN_HEADS // 2 + h // 2][:, (h % 2) * IDX_DIM:(h % 2 + 1) * IDX_DIM]
        qi_ref[h] = blk.astype(BF16)
    last = r[(ATTN_HEADS + IDX_HEADS) // 2]
    k_ref[...] = last[:, :HEAD_DIM].astype(BF16)
    ki_ref[...] = last[:, HEAD_DIM:].astype(BF16)
    vt = _dot_t(wvt_ref[...], xn)
    vrow = lax.broadcasted_iota(I32, vt.shape, 0)
    vt_ref[...] = jnp.where(vrow == HEAD_DIM, 1.0, vt).astype(BF16)
    wit_ref[...] = _dot_t(wwi_ref[...], xn)

    cu = u[:, 2 * rope_w:] + glub_ref[...]
    hh = cu[:, :conv_w] * jax.nn.sigmoid(cu[:, conv_w:])

    @pl.when(t == 0)
    def _():
        hbuf[0:CONV_HALO, :] = jnp.zeros((CONV_HALO, conv_w), F32)

    hbuf[CONV_HALO:CONV_HALO + tm, :] = hh
    base = CONV_HALO - (CONV_KERNEL - 1)
    for rb in range(tm // row_blk):
        acc = jnp.zeros((row_blk, conv_w), F32)
        for j in range(CONV_KERNEL):
            s0 = rb * row_blk + base + j
            acc = acc + hbuf[s0:s0 + row_blk, :] * dww_ref[j:j + 1, :]
        y = acc + dwb_ref[...]
        mu = jnp.mean(y, axis=-1, keepdims=True)
        yc = y - mu
        var = jnp.mean(yc * yc, axis=-1, keepdims=True)
        z = yc * lax.rsqrt(var + EPS) * lng_ref[...] + lnb_ref[...]
        conv_ref[rb * row_blk:(rb + 1) * row_blk, :] = (z * jax.nn.sigmoid(z)).astype(BF16)
    hbuf[0:CONV_HALO, :] = hbuf[tm:tm + CONV_HALO, :]


def _inproj(x, g_mix, w_all, w_wi_t, w_v_t, cos_t, sin_t, glu_b, dw_w, dw_b, ln_g, ln_b, tm):
    B, S, D = x.shape
    conv_w = dw_w.shape[1]
    nt = S // tm
    wtot = w_all.shape[1]
    row_blk = 64 if tm % 64 == 0 else tm
    kern = functools.partial(_inproj_kernel, tm=tm, conv_w=conv_w, row_blk=row_blk)
    full = lambda shape: pl.BlockSpec(shape, lambda b, t: (0,) * len(shape))
    out_shape = (
        jax.ShapeDtypeStruct((B, ATTN_HEADS, S, HEAD_DIM), BF16),
        jax.ShapeDtypeStruct((B, IDX_HEADS, S, IDX_DIM), BF16),
        jax.ShapeDtypeStruct((B, S, HEAD_DIM), BF16),
        jax.ShapeDtypeStruct((B, S, IDX_DIM), BF16),
        jax.ShapeDtypeStruct((B, nt, LANES, tm), BF16),
        jax.ShapeDtypeStruct((B, SUBLANES, S), F32),
        jax.ShapeDtypeStruct((B, S, conv_w), BF16),
    )
    out_specs = (
        pl.BlockSpec((None, ATTN_HEADS, tm, HEAD_DIM), lambda b, t: (b, 0, t, 0)),
        pl.BlockSpec((None, IDX_HEADS, tm, IDX_DIM), lambda b, t: (b, 0, t, 0)),
        pl.BlockSpec((None, tm, HEAD_DIM), lambda b, t: (b, t, 0)),
        pl.BlockSpec((None, tm, IDX_DIM), lambda b, t: (b, t, 0)),
        pl.BlockSpec((None, None, LANES, tm), lambda b, t: (b, t, 0, 0)),
        pl.BlockSpec((None, SUBLANES, tm), lambda b, t: (b, 0, t)),
        pl.BlockSpec((None, tm, conv_w), lambda b, t: (b, t, 0)),
    )
    in_specs = [
        pl.BlockSpec((None, tm, D), lambda b, t: (b, t, 0)),
        full((1, D)),
        full((D, wtot)),
        full((SUBLANES, D)),
        full((LANES, D)),
        pl.BlockSpec((tm, LANES), lambda b, t: (t, 0)),
        pl.BlockSpec((tm, LANES), lambda b, t: (t, 0)),
        full((1, 2 * conv_w)),
        full((CONV_KERNEL, conv_w)),
        full((1, conv_w)),
        full((1, conv_w)),
        full((1, conv_w)),
    ]
    return pl.pallas_call(
        kern, grid=(B, nt), in_specs=in_specs, out_specs=out_specs, out_shape=out_shape,
        scratch_shapes=[pltpu.VMEM((CONV_HALO + tm, conv_w), F32)],
        compiler_params=_params(2), name="inproj_rope_conv",
    )(x, g_mix, w_all, w_wi_t, w_v_t, cos_t, sin_t, glu_b, dw_w, dw_b, ln_g, ln_b)


def _count(pred, key_scr, nk, qb, sc):
    lanes_acc = COUNT_CHAINS * SUBLANES

    def chunk_body(c, acc):
        off = pl.multiple_of(c * sc, sc)
        keys3 = key_scr[pl.ds(off, sc), :].reshape(sc // SUBLANES, SUBLANES, qb)
        ones = jnp.where(pred(keys3, off), 1.0, 0.0).reshape(sc // lanes_acc, lanes_acc, qb)
        return acc + jnp.sum(ones, axis=0)
    acc = lax.fori_loop(0, nk, chunk_body, jnp.zeros((lanes_acc, qb), F32))
    acc = jnp.sum(acc.reshape(COUNT_CHAINS, SUBLANES, qb), axis=0)
    return jnp.broadcast_to(jnp.sum(acc, axis=0, keepdims=True), (SUBLANES, qb))


def _dsa_kernel(q_ref, qi_ref, wit_ref, k_ref, ki_ref, vt_ref, o_ref, key_scr, m_scr, acc_scr,
                *, qb, sc, n_sel, idx_bits, seq):
    j = pl.program_id(1)
    q_lo = j * qb
    nk = (q_lo + qb + sc - 1) // sc
    n_sel_f = jnp.float32(n_sel)
    q_pos = q_lo + lax.broadcasted_iota(I32, (sc, qb), 1)
    k_in_chunk = lax.broadcasted_iota(I32, (sc, qb), 0)
    k_in_chunk3 = k_in_chunk.reshape(sc // SUBLANES, SUBLANES, qb)
    q_pos3 = q_pos.reshape(sc // SUBLANES, SUBLANES, qb)

    w_eff = wit_ref[...] * (IDX_DIM ** -0.5 * IDX_HEADS ** -0.5)

    def score_body(c, carry):
        off = pl.multiple_of(c * sc, sc)
        kc = ki_ref[pl.ds(off, sc), :]
        sco = jnp.zeros((sc, qb), F32)
        for h in range(IDX_HEADS):
            sco = sco + jnp.maximum(_dot_t(kc, qi_ref[h]), 0.0) * w_eff[h:h + 1, :]
        sco = jnp.where(sco == 0.0, 0.0, sco)
        sco = jnp.where(off + k_in_chunk <= q_pos, sco, -jnp.inf)
        bits = pltpu.bitcast(sco, I32)
        key_scr[pl.ds(off, sc), :] = jnp.where(bits < 0, bits ^ jnp.int32(0x7FFFFFFF), bits)
        return carry

    lax.fori_loop(0, nk, score_body, 0)

    cnt0 = _count(lambda k3, off: k3 >= 0, key_scr, nk, qb, sc)
    thr = jnp.where(cnt0 >= n_sel_f, jnp.int32(0), jnp.int32(INT_MIN))

    def bit_body(i, thr):
        cand = thr | lax.shift_left(jnp.int32(1), jnp.int32(30) - i)
        cnt = _count(lambda k3, off: k3 >= cand[None], key_scr, nk, qb, sc)
        return jnp.where(cnt >= n_sel_f, cand, thr)

    thr = lax.fori_loop(0, 31, bit_body, thr)

    cnt_ge = _count(lambda k3, off: k3 >= thr[None], key_scr, nk, qb, sc)

    def tie_break():
        cnt_gt = _count(lambda k3, off: k3 > thr[None], key_scr, nk, qb, sc)
        need = n_sel_f - cnt_gt

        def idx_body(i, jj):
            cand = jj | lax.shift_left(jnp.int32(1), jnp.int32(idx_bits - 1) - i)
            cnt = _count(lambda k3, off: jnp.logical_and(k3 == thr[None], off + k_in_chunk3 < cand[None]),
                         key_scr, nk, qb, sc)
            return jnp.where(cnt < need, cand, jj)

        return lax.fori_loop(0, idx_bits, idx_body, jnp.zeros((SUBLANES, qb), I32))

    jmax = lax.cond(jnp.max(cnt_ge) > n_sel_f, tie_break, lambda: jnp.full((SUBLANES, qb), seq, I32))

    n_pairs = ATTN_HEADS // HEADS_PER_GROUP
    pw = HEADS_PER_GROUP * qb
    m_scr[...] = jnp.full(m_scr.shape, NEG_BIG, F32)
    acc_scr[...] = jnp.zeros(acc_scr.shape, F32)

    def attn_body(c, carry):
        off = pl.multiple_of(c * sc, sc)
        kc = k_ref[pl.ds(off, sc), :]
        vt = vt_ref[c]
        keys3 = key_scr[pl.ds(off, sc), :].reshape(sc // SUBLANES, SUBLANES, qb)
        kidx3 = off + k_in_chunk3
        tie = jnp.logical_and(keys3 == thr[None], kidx3 <= jmax[None])
        sel = jnp.logical_and(jnp.logical_or(keys3 > thr[None], tie), kidx3 <= q_pos3)
        bias = jnp.where(sel, 0.0, NEG_BIG)
        bias2 = jnp.concatenate([bias] * HEADS_PER_GROUP, axis=-1)
        for hp in range(n_pairs):
            qp = q_ref[HEADS_PER_GROUP * hp:HEADS_PER_GROUP * (hp + 1)].reshape(pw, HEAD_DIM)
            s3 = _dot_t(kc, qp).reshape(sc // SUBLANES, SUBLANES, pw) + bias2
            mx = jnp.max(s3, axis=0)
            mx = jnp.broadcast_to(jnp.max(mx, axis=0, keepdims=True), mx.shape)
            m_old = m_scr[hp]
            m_new = jnp.maximum(m_old, mx)
            alpha = jnp.exp2(m_old - m_new)
            p = jnp.exp2(s3 - m_new[None]).reshape(sc, pw).astype(BF16)
            pv = jnp.dot(vt, p, preferred_element_type=F32)
            acc3 = acc_scr[hp].reshape(LANES // SUBLANES, SUBLANES, pw)
            acc_scr[hp] = (acc3 * alpha[None]).reshape(LANES, pw) + pv
            m_scr[hp] = m_new
        return carry

    lax.fori_loop(0, nk, attn_body, 0)
    for hp in range(n_pairs):
        acc = acc_scr[hp]
        out_t = acc[:HEAD_DIM, :] / acc[HEAD_DIM:HEAD_DIM + 1, :]
        o_ref[HEADS_PER_GROUP * hp:HEADS_PER_GROUP * (hp + 1)] = (
            out_t.T.reshape(HEADS_PER_GROUP, qb, HEAD_DIM).astype(BF16))


def _dsa(q, qi, wit, k, ki, vt, qb, sc):
    B, H, S, dh = q.shape
    n_sel = min(TOPK_MAX, S // 4)
    idx_bits = max(1, int(np.ceil(np.log2(S))))
    assert S % sc == 0 and sc % qb == 0 and qb % LANES == 0 and vt.shape[1:] == (S // sc, LANES, sc)
    kern = functools.partial(_dsa_kernel, qb=qb, sc=sc, n_sel=n_sel, idx_bits=idx_bits, seq=S)
    in_specs = [
        pl.BlockSpec((None, H, qb, dh), lambda b, j: (b, 0, j, 0)),
        pl.BlockSpec((None, IDX_HEADS, qb, IDX_DIM), lambda b, j: (b, 0, j, 0)),
        pl.BlockSpec((None, SUBLANES, qb), lambda b, j: (b, 0, j)),
        pl.BlockSpec((None, S, dh), lambda b, j: (b, 0, 0)),
        pl.BlockSpec((None, S, IDX_DIM), lambda b, j: (b, 0, 0)),
        pl.BlockSpec((None, S // sc, LANES, sc), lambda b, j: (b, 0, 0, 0)),
    ]
    return pl.pallas_call(
        kern, grid=(B, S // qb), in_specs=in_specs,
        out_specs=pl.BlockSpec((None, H, qb, dh), lambda b, j: (b, 0, j, 0)),
        out_shape=jax.ShapeDtypeStruct((B, H, S, dh), BF16),
        scratch_shapes=[
            pltpu.VMEM((S, qb), I32),
            pltpu.VMEM((H // HEADS_PER_GROUP, SUBLANES, HEADS_PER_GROUP * qb), F32),
            pltpu.VMEM((H // HEADS_PER_GROUP, LANES, HEADS_PER_GROUP * qb), F32),
        ],
        compiler_params=_params(2), name="dsa_attention",
    )(q, qi, wit, k, ki, vt)


def _mix_router_kernel(x_ref, attn_ref, conv_ref, wout_ref, g_ref, wr_ref, br_ref,
                       h1_ref, xn_ref, route_ref, gate_ref, cnt_ref, carry, *, tm):
    i = pl.program_id(0)

    @pl.when(i == 0)
    def _():
        carry[...] = jnp.zeros(carry.shape, F32)

    attn_w = ATTN_HEADS * HEAD_DIM
    acc = x_ref[...] + jnp.dot(conv_ref[...], wout_ref[attn_w:, :], preferred_element_type=F32)
    for h in range(ATTN_HEADS):
        acc = acc + jnp.dot(attn_ref[h], wout_ref[h * HEAD_DIM:(h + 1) * HEAD_DIM, :],
                            preferred_element_type=F32)
    h1_ref[...] = acc
    xn = _rms(acc, g_ref[...])
    xn_ref[...] = xn
    logits = jnp.dot(xn.astype(BF16), wr_ref[...], preferred_element_type=F32) + br_ref[...]

    ne = logits.shape[1]
    eid = lax.broadcasted_iota(I32, (tm, ne), 1)
    work = logits
    vals, hots = [], []
    for _ in range(TOP_K):
        m = jnp.max(work, axis=-1, keepdims=True)
        e = jnp.min(jnp.where(work == m, eid, ne), axis=-1, keepdims=True)
        hot = eid == e
        work = jnp.where(hot, -jnp.inf, work)
        vals.append(m)
        hots.append(hot)
    ex = [jnp.exp(v - vals[0]) for v in vals]
    den = ex[0] + ex[1] + ex[2] + ex[3]

    multi = jnp.zeros((tm, ne), F32)
    for hot in hots:
        multi = multi + jnp.where(hot, 1.0, 0.0)
    ri = lax.broadcasted_iota(I32, (tm, tm), 0)
    ci = lax.broadcasted_iota(I32, (tm, tm), 1)
    tri = jnp.where(ci < ri, 1.0, 0.0).astype(BF16)
    before = jnp.dot(tri, multi.astype(BF16), preferred_element_type=F32) + carry[0:1, :ne]
    carry[0:1, :ne] = carry[0:1, :ne] + jnp.sum(multi, axis=0, keepdims=True)
    cnt_ref[...] = carry[...]

    lane8 = lax.broadcasted_iota(I32, (tm, 2 * TOP_K), 1)
    lane4 = lax.broadcasted_iota(I32, (tm, TOP_K), 1)
    route = jnp.zeros((tm, 2 * TOP_K), I32)
    gates = jnp.zeros((tm, TOP_K), F32)
    for kk in range(TOP_K):
        e = jnp.sum(jnp.where(hots[kk], eid, 0), axis=-1, keepdims=True)
        rk = jnp.sum(jnp.where(hots[kk], before, 0.0), axis=-1, keepdims=True).astype(I32)
        route = route + jnp.where(lane8 == kk, e, 0) + jnp.where(lane8 == TOP_K + kk, rk, 0)
        gates = gates + jnp.where(lane4 == kk, ex[kk] / den, 0.0)
    route_ref[...] = route
    gate_ref[...] = gates


def _mix_router(x2, attn, conv, w_out, g_ffn, w_router, b_router, tm):
    N, D = x2.shape
    B, H, S, dh = attn.shape
    nt = S // tm
    ne = w_router.shape[1]
    cw = conv.shape[-1]
    kern = functools.partial(_mix_router_kernel, tm=tm)
    full = lambda shape: pl.BlockSpec(shape, lambda i: (0,) * len(shape))
    in_specs = [
        pl.BlockSpec((tm, D), lambda i: (i, 0)),
        pl.BlockSpec((None, H, tm, dh), lambda i: (i // nt, 0, i % nt, 0)),
        pl.BlockSpec((None, tm, cw), lambda i: (i // nt, i % nt, 0)),
        full(w_out.shape), full((1, D)), full((D, ne)), full((1, ne)),
    ]
    out_shape = (
        jax.ShapeDtypeStruct((N, D), F32),
        jax.ShapeDtypeStruct((N, D), F32),
        jax.ShapeDtypeStruct((N, 2 * TOP_K), I32),
        jax.ShapeDtypeStruct((N, TOP_K), F32),
        jax.ShapeDtypeStruct((SUBLANES, LANES), F32),
    )
    out_specs = (
        pl.BlockSpec((tm, D), lambda i: (i, 0)),
        pl.BlockSpec((tm, D), lambda i: (i, 0)),
        pl.BlockSpec((tm, 2 * TOP_K), lambda i: (i, 0)),
        pl.BlockSpec((tm, TOP_K), lambda i: (i, 0)),
        pl.BlockSpec((SUBLANES, LANES), lambda i: (0, 0)),
    )
    return pl.pallas_call(
        kern, grid=(N // tm,), in_specs=in_specs, out_specs=out_specs, out_shape=out_shape,
        scratch_shapes=[pltpu.VMEM((SUBLANES, LANES), F32)],
        compiler_params=_params(1), name="outproj_router",
    )(x2, attn, conv, w_out, g_ffn, w_router, b_router)


def _row_copy(src_ref, src_row, dst_ref, dst_row, sem):
    return pltpu.make_async_copy(src_ref.at[pl.ds(src_row, 1)], dst_ref.at[pl.ds(dst_row, 1)], sem)


def _dispatch_kernel(zstart_ref, dest_ref, x_ref, out_ref, zbuf, sem, zsem, *, tb, tmx, n_blocks):
    zspan = zbuf.shape[0]

    @pl.when(pl.program_id(0) == 0)
    def _():
        zbuf[...] = jnp.zeros(zbuf.shape, zbuf.dtype)
        for e in range(N_EXPERTS):
            z0 = pl.multiple_of(zstart_ref[e], SUBLANES)
            pltpu.make_async_copy(zbuf, out_ref.at[pl.ds(z0, zspan)], zsem).start()
        for e in range(N_EXPERTS):
            pltpu.make_async_copy(zbuf, out_ref.at[pl.ds(0, zspan)], zsem).wait()

        def tail_copy(i):
            return pltpu.make_async_copy(zbuf.at[pl.ds(0, tmx)], out_ref.at[pl.ds(pl.multiple_of(i * tmx, tmx), tmx)],
                                         zsem)

        def tail_start(i, c):
            tail_copy(i).start()
            return c

        def tail_wait(i, c):
            tail_copy(i).wait()
            return c

        lax.fori_loop(zstart_ref[N_EXPERTS], n_blocks, tail_start, 0)
        lax.fori_loop(zstart_ref[N_EXPERTS], n_blocks, tail_wait, 0)

    def start(r, c):
        for kk in range(TOP_K):
            _row_copy(x_ref, r, out_ref, dest_ref[r * TOP_K + kk], sem).start(priority=kk % 2)
        return c

    lax.fori_loop(0, tb, start, 0)

    def wait(r, c):
        for kk in range(TOP_K):
            _row_copy(x_ref, 0, out_ref, 0, sem).wait()
        return c

    lax.fori_loop(0, tb, wait, 0)


def _dispatch(zstart, dest_flat, xn, n_rows, tb, tmx, zspan):
    N, D = xn.shape
    kern = functools.partial(_dispatch_kernel, tb=tb, tmx=tmx, n_blocks=n_rows // tmx)
    grid_spec = pltpu.PrefetchScalarGridSpec(
        num_scalar_prefetch=1, grid=(N // tb,),
        in_specs=[
            pl.BlockSpec((tb * TOP_K,), lambda i, zs: (i,), memory_space=pltpu.SMEM),
            pl.BlockSpec((tb, D), lambda i, zs: (i, 0)),
        ],
        out_specs=pl.BlockSpec(memory_space=pl.ANY),
        scratch_shapes=[pltpu.VMEM((zspan, D), xn.dtype), pltpu.SemaphoreType.DMA(()),
                        pltpu.SemaphoreType.DMA(())],
    )
    return pl.pallas_call(
        kern, grid_spec=grid_spec,
        out_shape=jax.ShapeDtypeStruct((n_rows, D), xn.dtype),
        compiler_params=_params(1), name="moe_dispatch",
    )(zstart, dest_flat, xn)


def _expert_kernel(bexp_ref, nused_ref, x_ref, wu_ref, bu_ref, wd_ref, bd_ref, y_ref, wu_bf, wd_bf):
    i = pl.program_id(0)
    used = i < nused_ref[0]

    @pl.when(jnp.logical_and(used, jnp.logical_or(i == 0, bexp_ref[i] != bexp_ref[jnp.maximum(i - 1, 0)])))
    def _():
        wu_bf[...] = wu_ref[...].astype(BF16)
        wd_bf[...] = wd_ref[...].astype(BF16)

    @pl.when(used)
    def _():
        de = wd_ref.shape[0]
        h = jnp.dot(x_ref[...].astype(BF16), wu_bf[...], preferred_element_type=F32) + bu_ref[...]
        gt = jnp.minimum(h[:, :de], SWIGLU_LIMIT)
        lin = jnp.clip(h[:, de:], -SWIGLU_LIMIT, SWIGLU_LIMIT)
        act = gt * jax.nn.sigmoid(SWIGLU_ALPHA * gt) * (lin + 1.0)
        y_ref[...] = jnp.dot(act.astype(BF16), wd_bf[...], preferred_element_type=F32) + bd_ref[...]

    @pl.when(i >= nused_ref[0])
    def _():
        y_ref[...] = jnp.zeros(y_ref.shape, F32)


def _experts(block_exp, n_used, xs, w_up, b_up, w_down, b_down, tmx):
    P, D = xs.shape
    E, _, two_de = w_up.shape
    de = two_de // 2
    grid_spec = pltpu.PrefetchScalarGridSpec(
        num_scalar_prefetch=2, grid=(P // tmx,),
        in_specs=[
            pl.BlockSpec((tmx, D), lambda i, be, nu: (jnp.minimum(i, nu[0] - 1), 0)),
            pl.BlockSpec((None, D, two_de), lambda i, be, nu: (be[i], 0, 0)),
            pl.BlockSpec((None, 1, two_de), lambda i, be, nu: (be[i], 0, 0)),
            pl.BlockSpec((None, de, D), lambda i, be, nu: (be[i], 0, 0)),
            pl.BlockSpec((None, 1, D), lambda i, be, nu: (be[i], 0, 0)),
        ],
        out_specs=pl.BlockSpec((tmx, D), lambda i, be, nu: (i, 0)),
        scratch_shapes=[pltpu.VMEM((D, two_de), BF16), pltpu.VMEM((de, D), BF16)],
    )
    return pl.pallas_call(
        _expert_kernel, grid_spec=grid_spec,
        out_shape=jax.ShapeDtypeStruct((P, D), F32),
        compiler_params=_params(1), name="moe_experts",
    )(block_exp, n_used, xs, w_up, b_up.reshape(E, 1, two_de), w_down, b_down.reshape(E, 1, D))


def _combine_kernel(dest_ref, h1_ref, gate_ref, p_ref, ys_ref, gple_ref, wg_ref, wp_ref, gfin_ref,
                    o_ref, gbuf, sem, *, tb):
    def start(r, c):
        for kk in range(TOP_K):
            _row_copy(ys_ref, dest_ref[r * TOP_K + kk], gbuf.at[kk], r, sem).start(priority=kk % 2)
        return c

    lax.fori_loop(0, tb, start, 0)

    def wait(r, c):
        for kk in range(TOP_K):
            _row_copy(ys_ref, 0, gbuf.at[kk], 0, sem).wait()
        return c

    lax.fori_loop(0, tb, wait, 0)

    g = gate_ref[...]
    h2 = h1_ref[...]
    for kk in range(TOP_K):
        h2 = h2 + g[:, kk:kk + 1] * gbuf[kk]
    z = jnp.dot(_rms(h2, gple_ref[...]).astype(BF16), wg_ref[...], preferred_element_type=F32)
    pe = jnp.dot(p_ref[...].astype(BF16), wp_ref[...], preferred_element_type=F32)
    h3 = h2 + pe * jax.nn.sigmoid(z)
    o_ref[...] = _rms(h3, gfin_ref[...])


def _combine(dest_flat, h1, gates, p2, ys, g_ple, w_gate, w_proj, g_final, tb):
    N, D = h1.shape
    pd = p2.shape[1]
    kern = functools.partial(_combine_kernel, tb=tb)
    full = lambda shape: pl.BlockSpec(shape, lambda i: (0,) * len(shape))
    return pl.pallas_call(
        kern, grid=(N // tb,),
        in_specs=[
            pl.BlockSpec((tb * TOP_K,), lambda i: (i,), memory_space=pltpu.SMEM),
            pl.BlockSpec((tb, D), lambda i: (i, 0)),
            pl.BlockSpec((tb, TOP_K), lambda i: (i, 0)),
            pl.BlockSpec((tb, pd), lambda i: (i, 0)),
            pl.BlockSpec(memory_space=pl.ANY),
            full((1, D)), full((D, D)), full((pd, D)), full((1, D)),
        ],
        out_specs=pl.BlockSpec((tb, D), lambda i: (i, 0)),
        out_shape=jax.ShapeDtypeStruct((N, D), F32),
        scratch_shapes=[pltpu.VMEM((TOP_K, tb, D), F32), pltpu.SemaphoreType.DMA(())],
        compiler_params=_params(1), name="combine_ple_norm",
    )(dest_flat, h1, gates, p2, ys, g_ple, w_gate, w_proj, g_final)


def _pick(n, prefs):
    for t in prefs:
        if n % t == 0:
            return t
    return n


def _rope_tables(S):
    inv = ROPE_THETA ** (-jnp.arange(0, HEAD_DIM, 2, dtype=F32) / HEAD_DIM)
    ang = jnp.arange(S, dtype=F32)[:, None] * inv[None, :]
    cos = jnp.cos(ang)
    sin = jnp.sin(ang)
    reps = LANES // HEAD_DIM
    return jnp.tile(jnp.concatenate([cos, cos], -1), (1, reps)), jnp.tile(jnp.concatenate([sin, sin], -1), (1, reps))


def _rot_cols(w, n_heads):
    D = w.shape[0]
    w3 = w.reshape(D, n_heads, 2, HEAD_DIM // 2)
    return jnp.stack([-w3[:, :, 1], w3[:, :, 0]], axis=2).reshape(D, n_heads * HEAD_DIM)


def _layer(h, p, w_in, w_out, g_mix, g_ffn, glu_b, dw_w, dw_b, ln_g, ln_b, w_router, b_router,
           w_up, b_up, w_down, b_down, w_ple_proj, w_ple_gate, g_ple, g_final):
    B, S, D = h.shape
    N = B * S
    aw = ATTN_HEADS * HEAD_DIM
    iw = IDX_HEADS * IDX_DIM
    o = np.cumsum([0, aw, HEAD_DIM, HEAD_DIM, iw, IDX_DIM, IDX_HEADS])
    wq, wk, wv, wqi, wki, wwi, wcu = (w_in[:, o[0]:o[1]], w_in[:, o[1]:o[2]], w_in[:, o[2]:o[3]],
                                      w_in[:, o[3]:o[4]], w_in[:, o[4]:o[5]], w_in[:, o[5]:o[6]], w_in[:, o[6]:])
    rope_w = jnp.concatenate([wq, wqi, wk, wki], axis=1)
    rope_rot = jnp.concatenate([_rot_cols(wq, ATTN_HEADS), _rot_cols(wqi, IDX_HEADS),
                                _rot_cols(wk, 1), _rot_cols(wki, 1)], axis=1)
    w_all = jnp.concatenate([rope_w, rope_rot, wcu], axis=1).astype(BF16)
    w_wi_t = jnp.concatenate([wwi.T, jnp.zeros((SUBLANES - IDX_HEADS, D), F32)], axis=0).astype(BF16)
    w_v_t = jnp.concatenate([wv.T, jnp.zeros((LANES - HEAD_DIM, D), F32)], axis=0).astype(BF16)
    cos_t, sin_t = _rope_tables(S)

    tm = _pick(S, (512, 256, 128))
    q, qi, k, ki, vt, wit, conv = _inproj(h, g_mix[None], w_all, w_wi_t, w_v_t, cos_t, sin_t, glu_b[None], dw_w,
                                          dw_b[None], ln_g[None], ln_b[None], tm)
    qb = _pick(S, (128,))
    attn = _dsa(q, qi, wit, k, ki, vt, qb, tm)

    h1, xn, route, gates, cnt = _mix_router(h.reshape(N, D), attn, conv, w_out.astype(BF16), g_ffn[None],
                                            w_router.astype(BF16), b_router[None], tm)

    tmx = 512
    counts = cnt[0, :N_EXPERTS].astype(I32)
    padded = (counts + tmx - 1) // tmx * tmx
    cum = jnp.cumsum(padded)
    pstart = cum - padded
    n_blocks = (N * TOP_K + N_EXPERTS * (tmx - 1) + tmx - 1) // tmx
    n_rows = n_blocks * tmx
    dest = (pstart[route[:, :TOP_K]] + route[:, TOP_K:]).reshape(N * TOP_K)
    n_used = (cum[-1] // tmx).astype(I32)
    blk_start = jnp.minimum(jnp.arange(n_blocks, dtype=I32), n_used - 1) * tmx
    block_exp = jnp.minimum(jnp.sum((blk_start[:, None] >= cum[None, :]).astype(I32), axis=1), N_EXPERTS - 1)
    zspan = tmx + SUBLANES
    zstart = (jnp.minimum(pstart + counts, n_rows - zspan) // SUBLANES * SUBLANES).astype(I32)
    zstart = jnp.concatenate([zstart, n_used.reshape(1)])

    tb = _pick(N, (256,))
    xs = _dispatch(zstart, dest, xn, n_rows, tb, tmx, zspan)
    ys = _experts(block_exp, n_used.reshape(1), xs, w_up, b_up, w_down, b_down, tmx)
    out = _combine(dest, h1, gates, p.reshape(N, -1), ys, g_ple[None], w_ple_gate.astype(BF16),
                   w_ple_proj.astype(BF16), g_final[None], tb)
    return out.reshape(B, S, D)


def kernel(x, p, w_in, w_out, g_mix, g_ffn, conv_glu_b, conv_dw_w, conv_dw_b, conv_ln_g, conv_ln_b, w_router,
           b_router, w_up, b_up, w_down, b_down, w_ple_proj, w_ple_gate, g_ple, g_final):
    assert p.shape[0] == 1, "single trunk layer"
    return _layer(x, p[0], w_in[0], w_out[0], g_mix[0], g_ffn[0], conv_glu_b[0], conv_dw_w[0], conv_dw_b[0],
                  conv_ln_g[0], conv_ln_b[0], w_router[0], b_router[0], w_up[0], b_up[0], w_down[0], b_down[0],
                  w_ple_proj[0], w_ple_gate[0], g_ple[0], g_final)
```

```python
import functools

import jax
import jax.numpy as jnp
import numpy as np
from jax import lax
from jax.experimental import pallas as pl
from jax.experimental.pallas import tpu as pltpu

F32 = jnp.float32
BF16 = jnp.bfloat16
I32 = jnp.int32

ATTN_HEADS = 8
HEAD_DIM = 64
IDX_HEADS = 4
IDX_DIM = 64
TOPK_MAX = 256
CONV_KERNEL = 31
N_EXPERTS = 32
TOP_K = 4
SWIGLU_LIMIT = 7.0
SWIGLU_ALPHA = 1.702
ROPE_THETA = 10000.0
EPS = 1e-6

LANES = 128
SUBLANES = 8
CONV_HALO = 32
NEG_BIG = -1e30
INT_MIN = -2147483648
COUNT_CHAINS = 8
LOG2_E = 1.4426950408889634
HEADS_PER_GROUP = 8
VMEM_LIMIT = 56 * 1024 * 1024


def _params(n_axes):
    return pltpu.CompilerParams(dimension_semantics=("arbitrary",) * n_axes,
                                vmem_limit_bytes=VMEM_LIMIT)


def _rms(x, g):
    return x * lax.rsqrt(jnp.mean(x * x, axis=-1, keepdims=True) + EPS) * g


def _dot_t(a, b):
    return lax.dot_general(a, b, (((1,), (1,)), ((), ())), preferred_element_type=F32)


def _inproj_kernel(x_ref, g_ref, w_ref, wwi_ref, wvt_ref, cos_ref, sin_ref, glub_ref, dww_ref, dwb_ref, lng_ref,
                   lnb_ref, q_ref, qi_ref, k_ref, ki_ref, vt_ref, wit_ref, conv_ref, hbuf, *, tm, conv_w, row_blk):
    t = pl.program_id(1)
    xn = _rms(x_ref[...], g_ref[...]).astype(BF16)
    u = jnp.dot(xn, w_ref[...], preferred_element_type=F32)
    rope_w = (ATTN_HEADS + IDX_HEADS + 2) * HEAD_DIM
    cos = cos_ref[...]
    sin = sin_ref[...]
    r = [u[:, c * LANES:(c + 1) * LANES] * cos + u[:, rope_w + c * LANES:rope_w + (c + 1) * LANES] * sin
         for c in range(rope_w // LANES)]
    for h in range(ATTN_HEADS):
        blk = r[h // 2][:, (h % 2) * HEAD_DIM:(h % 2 + 1) * HEAD_DIM]
        q_ref[h] = (blk * (HEAD_DIM ** -0.5 * LOG2_E)).astype(BF16)
    for h in range(IDX_HEADS):
        blk = r[ATTN_HEADS // 2 + h // 2][:, (h % 2) * IDX_DIM:(h % 2 + 1) * IDX_DIM]
        qi_ref[h] = blk.astype(BF16)
    last = r[(ATTN_HEADS + IDX_HEADS) // 2]
    k_ref[...] = last[:, :HEAD_DIM].astype(BF16)
    ki_ref[...] = last[:, HEAD_DIM:].astype(BF16)
    vt = _dot_t(wvt_ref[...], xn)
    vrow = lax.broadcasted_iota(I32, vt.shape, 0)
    vt_ref[...] = jnp.where(vrow == HEAD_DIM, 1.0, vt).astype(BF16)
    wit_ref[...] = _dot_t(wwi_ref[...], xn)

    cu = u[:, 2 * rope_w:] + glub_ref[...]
    hh = cu[:, :conv_w] * jax.nn.sigmoid(cu[:, conv_w:])

    @pl.when(t == 0)
    def _():
        hbuf[0:CONV_HALO, :] = jnp.zeros((CONV_HALO, conv_w), F32)

    hbuf[CONV_HALO:CONV_HALO + tm, :] = hh
    base = CONV_HALO - (CONV_KERNEL - 1)
    for rb in range(tm // row_blk):
        acc = jnp.zeros((row_blk, conv_w), F32)
        for j in range(CONV_KERNEL):
            s0 = rb * row_blk + base + j
            acc = acc + hbuf[s0:s0 + row_blk, :] * dww_ref[j:j + 1, :]
        y = acc + dwb_ref[...]
        mu = jnp.mean(y, axis=-1, keepdims=True)
        yc = y - mu
        var = jnp.mean(yc * yc, axis=-1, keepdims=True)
        z = yc * lax.rsqrt(var + EPS) * lng_ref[...] + lnb_ref[...]
        conv_ref[rb * row_blk:(rb + 1) * row_blk, :] = (z * jax.nn.sigmoid(z)).astype(BF16)
    hbuf[0:CONV_HALO, :] = hbuf[tm:tm + CONV_HALO, :]


def _inproj(x, g_mix, w_all, w_wi_t, w_v_t, cos_t, sin_t, glu_b, dw_w, dw_b, ln_g, ln_b, tm):
    B, S, D = x.shape
    conv_w = dw_w.shape[1]
    nt = S // tm
    wtot = w_all.shape[1]
    row_blk = 64 if tm % 64 == 0 else tm
    kern = functools.partial(_inproj_kernel, tm=tm, conv_w=conv_w, row_blk=row_blk)
    full = lambda shape: pl.BlockSpec(shape, lambda b, t: (0,) * len(shape))
    out_shape = (
        jax.ShapeDtypeStruct((B, ATTN_HEADS, S, HEAD_DIM), BF16),
        jax.ShapeDtypeStruct((B, IDX_HEADS, S, IDX_DIM), BF16),
        jax.ShapeDtypeStruct((B, S, HEAD_DIM), BF16),
        jax.ShapeDtypeStruct((B, S, IDX_DIM), BF16),
        jax.ShapeDtypeStruct((B, nt, LANES, tm), BF16),
        jax.ShapeDtypeStruct((B, SUBLANES, S), F32),
        jax.ShapeDtypeStruct((B, S, conv_w), BF16),
    )
    out_specs = (
        pl.BlockSpec((None, ATTN_HEADS, tm, HEAD_DIM), lambda b, t: (b, 0, t, 0)),
        pl.BlockSpec((None, IDX_HEADS, tm, IDX_DIM), lambda b, t: (b, 0, t, 0)),
        pl.BlockSpec((None, tm, HEAD_DIM), lambda b, t: (b, t, 0)),
        pl.BlockSpec((None, tm, IDX_DIM), lambda b, t: (b, t, 0)),
        pl.BlockSpec((None, None, LANES, tm), lambda b, t: (b, t, 0, 0)),
        pl.BlockSpec((None, SUBLANES, tm), lambda b, t: (b, 0, t)),
        pl.BlockSpec((None, tm, conv_w), lambda b, t: (b, t, 0)),
    )
    in_specs = [
        pl.BlockSpec((None, tm, D), lambda b, t: (b, t, 0)),
        full((1, D)),
        full((D, wtot)),
        full((SUBLANES, D)),
        full((LANES, D)),
        pl.BlockSpec((tm, LANES), lambda b, t: (t, 0)),
        pl.BlockSpec((tm, LANES), lambda b, t: (t, 0)),
        full((1, 2 * conv_w)),
        full((CONV_KERNEL, conv_w)),
        full((1, conv_w)),
        full((1, conv_w)),
        full((1, conv_w)),
    ]
    return pl.pallas_call(
        kern, grid=(B, nt), in_specs=in_specs, out_specs=out_specs, out_shape=out_shape,
        scratch_shapes=[pltpu.VMEM((CONV_HALO + tm, conv_w), F32)],
        compiler_params=_params(2), name="inproj_rope_conv",
    )(x, g_mix, w_all, w_wi_t, w_v_t, cos_t, sin_t, glu_b, dw_w, dw_b, ln_g, ln_b)


def _count(pred, key_scr, n_chunks, qb, sc):
    lanes_acc = COUNT_CHAINS * SUBLANES
    acc = jnp.zeros((lanes_acc, qb), F32)
    for c in range(n_chunks):
        keys3 = key_scr[c * sc:(c + 1) * sc, :].reshape(sc // SUBLANES, SUBLANES, qb)
        ones = jnp.where(pred(keys3, c * sc), 1.0, 0.0).reshape(sc // lanes_acc, lanes_acc, qb)
        acc = acc + jnp.sum(ones, axis=0)
    acc = jnp.sum(acc.reshape(COUNT_CHAINS, SUBLANES, qb), axis=0)
    return jnp.broadcast_to(jnp.sum(acc, axis=0, keepdims=True), (SUBLANES, qb))


def _select(key_scr, n_chunks, k_in_chunk3, *, qb, sc, n_sel, idx_bits, seq):
    n_sel_f = jnp.float32(n_sel)
    count = functools.partial(_count, key_scr=key_scr, n_chunks=n_chunks, qb=qb, sc=sc)

    cnt0 = count(lambda k3, off: k3 >= 0)
    thr = jnp.where(cnt0 >= n_sel_f, jnp.int32(0), jnp.int32(INT_MIN))

    def bit_body(i, thr):
        cand = thr | lax.shift_left(jnp.int32(1), jnp.int32(30) - i)
        cnt = count(lambda k3, off: k3 >= cand[None])
        return jnp.where(cnt >= n_sel_f, cand, thr)

    thr = lax.fori_loop(0, 31, bit_body, thr)

    cnt_ge = count(lambda k3, off: k3 >= thr[None])

    def tie_break():
        need = n_sel_f - count(lambda k3, off: k3 > thr[None])

        def idx_body(i, jj):
            cand = jj | lax.shift_left(jnp.int32(1), jnp.int32(idx_bits - 1) - i)
            cnt = count(lambda k3, off: jnp.logical_and(k3 == thr[None], off + k_in_chunk3 < cand[None]))
            return jnp.where(cnt < need, cand, jj)

        return lax.fori_loop(0, idx_bits, idx_body, jnp.zeros((SUBLANES, qb), I32))

    jmax = lax.cond(jnp.max(cnt_ge) > n_sel_f, tie_break, lambda: jnp.full((SUBLANES, qb), seq, I32))
    return thr, jmax


def _dsa_kernel(q_ref, qi_ref, wit_ref, k_ref, ki_ref, vt_ref, o_ref, key_scr, sel_scr, m_scr, acc_scr,
                *, qb, sc, n_sel, idx_bits, seq):
    j = pl.program_id(1)
    q_lo = j * qb
    nk = (q_lo + qb + sc - 1) // sc
    q_pos = q_lo + lax.broadcasted_iota(I32, (sc, qb), 1)
    k_in_chunk = lax.broadcasted_iota(I32, (sc, qb), 0)
    k_in_chunk3 = k_in_chunk.reshape(sc // SUBLANES, SUBLANES, qb)
    q_pos3 = q_pos.reshape(sc // SUBLANES, SUBLANES, qb)

    w_eff = wit_ref[...] * (IDX_DIM ** -0.5 * IDX_HEADS ** -0.5)

    def score_body(c, carry):
        off = pl.multiple_of(c * sc, sc)
        kc = ki_ref[pl.ds(off, sc), :]
        sco = jnp.zeros((sc, qb), F32)
        for h in range(IDX_HEADS):
            sco = sco + jnp.maximum(_dot_t(kc, qi_ref[h]), 0.0) * w_eff[h:h + 1, :]
        sco = jnp.where(sco == 0.0, 0.0, sco)
        sco = jnp.where(off + k_in_chunk <= q_pos, sco, -jnp.inf)
        bits = pltpu.bitcast(sco, I32)
        key_scr[pl.ds(off, sc), :] = jnp.where(bits < 0, bits ^ jnp.int32(0x7FFFFFFF), bits)
        return carry

    lax.fori_loop(0, nk, score_body, 0)

    for n_chunks in range(1, seq // sc + 1):
        @pl.when(nk == n_chunks)
        def _(n_chunks=n_chunks):
            t, jm = _select(key_scr, n_chunks, k_in_chunk3, qb=qb, sc=sc, n_sel=n_sel, idx_bits=idx_bits, seq=seq)
            sel_scr[0] = t
            sel_scr[1] = jm

    thr = sel_scr[0]
    jmax = sel_scr[1]

    n_pairs = ATTN_HEADS // HEADS_PER_GROUP
    pw = HEADS_PER_GROUP * qb
    m_scr[...] = jnp.full(m_scr.shape, NEG_BIG, F32)
    acc_scr[...] = jnp.zeros(acc_scr.shape, F32)

    def attn_body(c, carry):
        off = pl.multiple_of(c * sc, sc)
        kc = k_ref[pl.ds(off, sc), :]
        vt = vt_ref[c]
        keys3 = key_scr[pl.ds(off, sc), :].reshape(sc // SUBLANES, SUBLANES, qb)
        kidx3 = off + k_in_chunk3
        tie = jnp.logical_and(keys3 == thr[None], kidx3 <= jmax[None])
        sel = jnp.logical_and(jnp.logical_or(keys3 > thr[None], tie), kidx3 <= q_pos3)
        bias = jnp.where(sel, 0.0, NEG_BIG)
        bias2 = jnp.concatenate([bias] * HEADS_PER_GROUP, axis=-1)
        for hp in range(n_pairs):
            qp = q_ref[HEADS_PER_GROUP * hp:HEADS_PER_GROUP * (hp + 1)].reshape(pw, HEAD_DIM)
            s3 = _dot_t(kc, qp).reshape(sc // SUBLANES, SUBLANES, pw) + bias2
            mx = jnp.max(s3, axis=0)
            mx = jnp.broadcast_to(jnp.max(mx, axis=0, keepdims=True), mx.shape)
            m_old = m_scr[hp]
            m_new = jnp.maximum(m_old, mx)
            alpha = jnp.exp2(m_old - m_new)
            p = jnp.exp2(s3 - m_new[None]).reshape(sc, pw).astype(BF16)
            pv = jnp.dot(vt, p, preferred_element_type=F32)
            acc3 = acc_scr[hp].reshape(LANES // SUBLANES, SUBLANES, pw)
            acc_scr[hp] = (acc3 * alpha[None]).reshape(LANES, pw) + pv
            m_scr[hp] = m_new
        return carry

    lax.fori_loop(0, nk, attn_body, 0)
    for hp in range(n_pairs):
        acc = acc_scr[hp]
        out_t = acc[:HEAD_DIM, :] / acc[HEAD_DIM:HEAD_DIM + 1, :]
        o_ref[HEADS_PER_GROUP * hp:HEADS_PER_GROUP * (hp + 1)] = (
            out_t.T.reshape(HEADS_PER_GROUP, qb, HEAD_DIM).astype(BF16))


def _dsa(q, qi, wit, k, ki, vt, qb, sc):
    B, H, S, dh = q.shape
    n_sel = min(TOPK_MAX, S // 4)
    idx_bits = max(1, int(np.ceil(np.log2(S))))
    assert S % sc == 0 and sc % qb == 0 and qb % LANES == 0 and vt.shape[1:] == (S // sc, LANES, sc)
    kern = functools.partial(_dsa_kernel, qb=qb, sc=sc, n_sel=n_sel, idx_bits=idx_bits, seq=S)
    in_specs = [
        pl.BlockSpec((None, H, qb, dh), lambda b, j: (b, 0, j, 0)),
        pl.BlockSpec((None, IDX_HEADS, qb, IDX_DIM), lambda b, j: (b, 0, j, 0)),
        pl.BlockSpec((None, SUBLANES, qb), lambda b, j: (b, 0, j)),
        pl.BlockSpec((None, S, dh), lambda b, j: (b, 0, 0)),
        pl.BlockSpec((None, S, IDX_DIM), lambda b, j: (b, 0, 0)),
        pl.BlockSpec((None, S // sc, LANES, sc), lambda b, j: (b, 0, 0, 0)),
    ]
    return pl.pallas_call(
        kern, grid=(B, S // qb), in_specs=in_specs,
        out_specs=pl.BlockSpec((None, H, qb, dh), lambda b, j: (b, 0, j, 0)),
        out_shape=jax.ShapeDtypeStruct((B, H, S, dh), BF16),
        scratch_shapes=[
            pltpu.VMEM((S, qb), I32),
            pltpu.VMEM((2, SUBLANES, qb), I32),
            pltpu.VMEM((H // HEADS_PER_GROUP, SUBLANES, HEADS_PER_GROUP * qb), F32),
            pltpu.VMEM((H // HEADS_PER_GROUP, LANES, HEADS_PER_GROUP * qb), F32),
        ],
        compiler_params=_params(2), name="dsa_attention",
    )(q, qi, wit, k, ki, vt)


def _mix_router_kernel(x_ref, attn_ref, conv_ref, wout_ref, g_ref, wr_ref, br_ref,
                       h1_ref, xn_ref, route_ref, gate_ref, cnt_ref, carry, *, tm):
    i = pl.program_id(0)

    @pl.when(i == 0)
    def _():
        carry[...] = jnp.zeros(carry.shape, F32)

    attn_w = ATTN_HEADS * HEAD_DIM
    acc = x_ref[...] + jnp.dot(conv_ref[...], wout_ref[attn_w:, :], preferred_element_type=F32)
    for h in range(ATTN_HEADS):
        acc = acc + jnp.dot(attn_ref[h], wout_ref[h * HEAD_DIM:(h + 1) * HEAD_DIM, :],
                            preferred_element_type=F32)
    h1_ref[...] = acc
    xn = _rms(acc, g_ref[...])
    xn_ref[...] = xn
    logits = jnp.dot(xn.astype(BF16), wr_ref[...], preferred_element_type=F32) + br_ref[...]

    ne = logits.shape[1]
    eid = lax.broadcasted_iota(I32, (tm, ne), 1)
    work = logits
    vals, hots = [], []
    for _ in range(TOP_K):
        m = jnp.max(work, axis=-1, keepdims=True)
        e = jnp.min(jnp.where(work == m, eid, ne), axis=-1, keepdims=True)
        hot = eid == e
        work = jnp.where(hot, -jnp.inf, work)
        vals.append(m)
        hots.append(hot)
    ex = [jnp.exp(v - vals[0]) for v in vals]
    den = ex[0] + ex[1] + ex[2] + ex[3]

    multi = jnp.zeros((tm, ne), F32)
    for hot in hots:
        multi = multi + jnp.where(hot, 1.0, 0.0)
    ri = lax.broadcasted_iota(I32, (tm, tm), 0)
    ci = lax.broadcasted_iota(I32, (tm, tm), 1)
    tri = jnp.where(ci < ri, 1.0, 0.0).astype(BF16)
    before = jnp.dot(tri, multi.astype(BF16), preferred_element_type=F32) + carry[0:1, :ne]
    carry[0:1, :ne] = carry[0:1, :ne] + jnp.sum(multi, axis=0, keepdims=True)
    cnt_ref[...] = carry[...]

    lane8 = lax.broadcasted_iota(I32, (tm, 2 * TOP_K), 1)
    lane4 = lax.broadcasted_iota(I32, (tm, TOP_K), 1)
    route = jnp.zeros((tm, 2 * TOP_K), I32)
    gates = jnp.zeros((tm, TOP_K), F32)
    for kk in range(TOP_K):
        e = jnp.sum(jnp.where(hots[kk], eid, 0), axis=-1, keepdims=True)
        rk = jnp.sum(jnp.where(hots[kk], before, 0.0), axis=-1, keepdims=True).astype(I32)
        route = route + jnp.where(lane8 == kk, e, 0) + jnp.where(lane8 == TOP_K + kk, rk, 0)
        gates = gates + jnp.where(lane4 == kk, ex[kk] / den, 0.0)
    route_ref[...] = route
    gate_ref[...] = gates


def _mix_router(x2, attn, conv, w_out, g_ffn, w_router, b_router, tm):
    N, D = x2.shape
    B, H, S, dh = attn.shape
    nt = S // tm
    ne = w_router.shape[1]
    cw = conv.shape[-1]
    kern = functools.partial(_mix_router_kernel, tm=tm)
    full = lambda shape: pl.BlockSpec(shape, lambda i: (0,) * len(shape))
    in_specs = [
        pl.BlockSpec((tm, D), lambda i: (i, 0)),
        pl.BlockSpec((None, H, tm, dh), lambda i: (i // nt, 0, i % nt, 0)),
        pl.BlockSpec((None, tm, cw), lambda i: (i // nt, i % nt, 0)),
        full(w_out.shape), full((1, D)), full((D, ne)), full((1, ne)),
    ]
    out_shape = (
        jax.ShapeDtypeStruct((N, D), F32),
        jax.ShapeDtypeStruct((N, D), F32),
        jax.ShapeDtypeStruct((N, 2 * TOP_K), I32),
        jax.ShapeDtypeStruct((N, TOP_K), F32),
        jax.ShapeDtypeStruct((SUBLANES, LANES), F32),
    )
    out_specs = (
        pl.BlockSpec((tm, D), lambda i: (i, 0)),
        pl.BlockSpec((tm, D), lambda i: (i, 0)),
        pl.BlockSpec((tm, 2 * TOP_K), lambda i: (i, 0)),
        pl.BlockSpec((tm, TOP_K), lambda i: (i, 0)),
        pl.BlockSpec((SUBLANES, LANES), lambda i: (0, 0)),
    )
    return pl.pallas_call(
        kern, grid=(N // tm,), in_specs=in_specs, out_specs=out_specs, out_shape=out_shape,
        scratch_shapes=[pltpu.VMEM((SUBLANES, LANES), F32)],
        compiler_params=_params(1), name="outproj_router",
    )(x2, attn, conv, w_out, g_ffn, w_router, b_router)


def _row_copy(src_ref, src_row, dst_ref, dst_row, sem):
    return pltpu.make_async_copy(src_ref.at[pl.ds(src_row, 1)], dst_ref.at[pl.ds(dst_row, 1)], sem)


def _dispatch_kernel(zstart_ref, dest_ref, x_ref, out_ref, zbuf, sem, zsem, *, tb, tmx, n_blocks):
    zspan = zbuf.shape[0]

    @pl.when(pl.program_id(0) == 0)
    def _():
        zbuf[...] = jnp.zeros(zbuf.shape, zbuf.dtype)
        for e in range(N_EXPERTS):
            z0 = pl.multiple_of(zstart_ref[e], SUBLANES)
            pltpu.make_async_copy(zbuf, out_ref.at[pl.ds(z0, zspan)], zsem).start()
        for e in range(N_EXPERTS):
            pltpu.make_async_copy(zbuf, out_ref.at[pl.ds(0, zspan)], zsem).wait()

        def tail_copy(i):
            return pltpu.make_async_copy(zbuf.at[pl.ds(0, tmx)], out_ref.at[pl.ds(pl.multiple_of(i * tmx, tmx), tmx)],
                                         zsem)

        def tail_start(i, c):
            tail_copy(i).start()
            return c

        def tail_wait(i, c):
            tail_copy(i).wait()
            return c

        lax.fori_loop(zstart_ref[N_EXPERTS], n_blocks, tail_start, 0)
        lax.fori_loop(zstart_ref[N_EXPERTS], n_blocks, tail_wait, 0)

    def start(r, c):
        for kk in range(TOP_K):
            _row_copy(x_ref, r, out_ref, dest_ref[r * TOP_K + kk], sem).start()
        return c

    lax.fori_loop(0, tb, start, 0)

    def wait(r, c):
        for kk in range(TOP_K):
            _row_copy(x_ref, 0, out_ref, 0, sem).wait()
        return c

    lax.fori_loop(0, tb, wait, 0)


def _dispatch(zstart, dest_flat, xn, n_rows, tb, tmx, zspan):
    N, D = xn.shape
    kern = functools.partial(_dispatch_kernel, tb=tb, tmx=tmx, n_blocks=n_rows // tmx)
    grid_spec = pltpu.PrefetchScalarGridSpec(
        num_scalar_prefetch=1, grid=(N // tb,),
        in_specs=[
            pl.BlockSpec((tb * TOP_K,), lambda i, zs: (i,), memory_space=pltpu.SMEM),
            pl.BlockSpec((tb, D), lambda i, zs: (i, 0)),
        ],
        out_specs=pl.BlockSpec(memory_space=pl.ANY),
        scratch_shapes=[pltpu.VMEM((zspan, D), xn.dtype), pltpu.SemaphoreType.DMA(()),
                        pltpu.SemaphoreType.DMA(())],
    )
    return pl.pallas_call(
        kern, grid_spec=grid_spec,
        out_shape=jax.ShapeDtypeStruct((n_rows, D), xn.dtype),
        compiler_params=_params(1), name="moe_dispatch",
    )(zstart, dest_flat, xn)


def _expert_kernel(bexp_ref, nused_ref, x_ref, wu_ref, bu_ref, wd_ref, bd_ref, y_ref, wu_bf, wd_bf):
    i = pl.program_id(0)
    used = i < nused_ref[0]

    @pl.when(jnp.logical_and(used, jnp.logical_or(i == 0, bexp_ref[i] != bexp_ref[jnp.maximum(i - 1, 0)])))
    def _():
        wu_bf[...] = wu_ref[...].astype(BF16)
        wd_bf[...] = wd_ref[...].astype(BF16)

    @pl.when(used)
    def _():
        de = wd_ref.shape[0]
        h = jnp.dot(x_ref[...].astype(BF16), wu_bf[...], preferred_element_type=F32) + bu_ref[...]
        gt = jnp.minimum(h[:, :de], SWIGLU_LIMIT)
        lin = jnp.clip(h[:, de:], -SWIGLU_LIMIT, SWIGLU_LIMIT)
        act = gt * jax.nn.sigmoid(SWIGLU_ALPHA * gt) * (lin + 1.0)
        y_ref[...] = jnp.dot(act.astype(BF16), wd_bf[...], preferred_element_type=F32) + bd_ref[...]

    @pl.when(i >= nused_ref[0])
    def _():
        y_ref[...] = jnp.zeros(y_ref.shape, F32)


def _experts(block_exp, n_used, xs, w_up, b_up, w_down, b_down, tmx):
    P, D = xs.shape
    E, _, two_de = w_up.shape
    de = two_de // 2
    grid_spec = pltpu.PrefetchScalarGridSpec(
        num_scalar_prefetch=2, grid=(P // tmx,),
        in_specs=[
            pl.BlockSpec((tmx, D), lambda i, be, nu: (jnp.minimum(i, nu[0] - 1), 0)),
            pl.BlockSpec((None, D, two_de), lambda i, be, nu: (be[i], 0, 0)),
            pl.BlockSpec((None, 1, two_de), lambda i, be, nu: (be[i], 0, 0)),
            pl.BlockSpec((None, de, D), lambda i, be, nu: (be[i], 0, 0)),
            pl.BlockSpec((None, 1, D), lambda i, be, nu: (be[i], 0, 0)),
        ],
        out_specs=pl.BlockSpec((tmx, D), lambda i, be, nu: (i, 0)),
        scratch_shapes=[pltpu.VMEM((D, two_de), BF16), pltpu.VMEM((de, D), BF16)],
    )
    return pl.pallas_call(
        _expert_kernel, grid_spec=grid_spec,
        out_shape=jax.ShapeDtypeStruct((P, D), F32),
        compiler_params=_params(1), name="moe_experts",
    )(block_exp, n_used, xs, w_up, b_up.reshape(E, 1, two_de), w_down, b_down.reshape(E, 1, D))


def _combine_kernel(dest_ref, h1_ref, gate_ref, p_ref, ys_ref, gple_ref, wg_ref, wp_ref, gfin_ref,
                    o_ref, gbuf, sem, *, tb):
    def start(r, c):
        for kk in range(TOP_K):
            _row_copy(ys_ref, dest_ref[r * TOP_K + kk], gbuf.at[kk], r, sem).start()
        return c

    lax.fori_loop(0, tb, start, 0)

    def wait(r, c):
        for kk in range(TOP_K):
            _row_copy(ys_ref, 0, gbuf.at[kk], 0, sem).wait()
        return c

    lax.fori_loop(0, tb, wait, 0)

    g = gate_ref[...]
    h2 = h1_ref[...]
    for kk in range(TOP_K):
        h2 = h2 + g[:, kk:kk + 1] * gbuf[kk]
    z = jnp.dot(_rms(h2, gple_ref[...]).astype(BF16), wg_ref[...], preferred_element_type=F32)
    pe = jnp.dot(p_ref[...].astype(BF16), wp_ref[...], preferred_element_type=F32)
    h3 = h2 + pe * jax.nn.sigmoid(z)
    o_ref[...] = _rms(h3, gfin_ref[...])


def _combine(dest_flat, h1, gates, p2, ys, g_ple, w_gate, w_proj, g_final, tb):
    N, D = h1.shape
    pd = p2.shape[1]
    kern = functools.partial(_combine_kernel, tb=tb)
    full = lambda shape: pl.BlockSpec(shape, lambda i: (0,) * len(shape))
    return pl.pallas_call(
        kern, grid=(N // tb,),
        in_specs=[
            pl.BlockSpec((tb * TOP_K,), lambda i: (i,), memory_space=pltpu.SMEM),
            pl.BlockSpec((tb, D), lambda i: (i, 0)),
            pl.BlockSpec((tb, TOP_K), lambda i: (i, 0)),
            pl.BlockSpec((tb, pd), lambda i: (i, 0)),
            pl.BlockSpec(memory_space=pl.ANY),
            full((1, D)), full((D, D)), full((pd, D)), full((1, D)),
        ],
        out_specs=pl.BlockSpec((tb, D), lambda i: (i, 0)),
        out_shape=jax.ShapeDtypeStruct((N, D), F32),
        scratch_shapes=[pltpu.VMEM((TOP_K, tb, D), F32), pltpu.SemaphoreType.DMA(())],
        compiler_params=_params(1), name="combine_ple_norm",
    )(dest_flat, h1, gates, p2, ys, g_ple, w_gate, w_proj, g_final)


def _pick(n, prefs):
    for t in prefs:
        if n % t == 0:
            return t
    return n


def _rope_tables(S):
    inv = ROPE_THETA ** (-jnp.arange(0, HEAD_DIM, 2, dtype=F32) / HEAD_DIM)
    ang = jnp.arange(S, dtype=F32)[:, None] * inv[None, :]
    cos = jnp.cos(ang)
    sin = jnp.sin(ang)
    reps = LANES // HEAD_DIM
    return jnp.tile(jnp.concatenate([cos, cos], -1), (1, reps)), jnp.tile(jnp.concatenate([sin, sin], -1), (1, reps))


def _rot_cols(w, n_heads):
    D = w.shape[0]
    w3 = w.reshape(D, n_heads, 2, HEAD_DIM // 2)
    return jnp.stack([-w3[:, :, 1], w3[:, :, 0]], axis=2).reshape(D, n_heads * HEAD_DIM)


def _layer(h, p, w_in, w_out, g_mix, g_ffn, glu_b, dw_w, dw_b, ln_g, ln_b, w_router, b_router,
           w_up, b_up, w_down, b_down, w_ple_proj, w_ple_gate, g_ple, g_final):
    B, S, D = h.shape
    N = B * S
    aw = ATTN_HEADS * HEAD_DIM
    iw = IDX_HEADS * IDX_DIM
    o = np.cumsum([0, aw, HEAD_DIM, HEAD_DIM, iw, IDX_DIM, IDX_HEADS])
    wq, wk, wv, wqi, wki, wwi, wcu = (w_in[:, o[0]:o[1]], w_in[:, o[1]:o[2]], w_in[:, o[2]:o[3]],
                                      w_in[:, o[3]:o[4]], w_in[:, o[4]:o[5]], w_in[:, o[5]:o[6]], w_in[:, o[6]:])
    rope_w = jnp.concatenate([wq, wqi, wk, wki], axis=1)
    rope_rot = jnp.concatenate([_rot_cols(wq, ATTN_HEADS), _rot_cols(wqi, IDX_HEADS),
                                _rot_cols(wk, 1), _rot_cols(wki, 1)], axis=1)
    w_all = jnp.concatenate([rope_w, rope_rot, wcu], axis=1).astype(BF16)
    w_wi_t = jnp.concatenate([wwi.T, jnp.zeros((SUBLANES - IDX_HEADS, D), F32)], axis=0).astype(BF16)
    w_v_t = jnp.concatenate([wv.T, jnp.zeros((LANES - HEAD_DIM, D), F32)], axis=0).astype(BF16)
    cos_t, sin_t = _rope_tables(S)

    tm = _pick(S, (512, 256, 128))
    q, qi, k, ki, vt, wit, conv = _inproj(h, g_mix[None], w_all, w_wi_t, w_v_t, cos_t, sin_t, glu_b[None], dw_w,
                                          dw_b[None], ln_g[None], ln_b[None], tm)
    qb = _pick(S, (128,))
    attn = _dsa(q, qi, wit, k, ki, vt, qb, tm)

    h1, xn, route, gates, cnt = _mix_router(h.reshape(N, D), attn, conv, w_out.astype(BF16), g_ffn[None],
                                            w_router.astype(BF16), b_router[None], tm)

    tmx = 512
    counts = cnt[0, :N_EXPERTS].astype(I32)
    padded = (counts + tmx - 1) // tmx * tmx
    cum = jnp.cumsum(padded)
    pstart = cum - padded
    n_blocks = (N * TOP_K + N_EXPERTS * (tmx - 1) + tmx - 1) // tmx
    n_rows = n_blocks * tmx
    dest = (pstart[route[:, :TOP_K]] + route[:, TOP_K:]).reshape(N * TOP_K)
    n_used = (cum[-1] // tmx).astype(I32)
    blk_start = jnp.minimum(jnp.arange(n_blocks, dtype=I32), n_used - 1) * tmx
    block_exp = jnp.minimum(jnp.sum((blk_start[:, None] >= cum[None, :]).astype(I32), axis=1), N_EXPERTS - 1)
    zspan = tmx + SUBLANES
    zstart = (jnp.minimum(pstart + counts, n_rows - zspan) // SUBLANES * SUBLANES).astype(I32)
    zstart = jnp.concatenate([zstart, n_used.reshape(1)])

    tb = _pick(N, (256,))
    xs = _dispatch(zstart, dest, xn, n_rows, tb, tmx, zspan)
    ys = _experts(block_exp, n_used.reshape(1), xs, w_up, b_up, w_down, b_down, tmx)
    out = _combine(dest, h1, gates, p.reshape(N, -1), ys, g_ple[None], w_ple_gate.astype(BF16),
                   w_ple_proj.astype(BF16), g_final[None], tb)
    return out.reshape(B, S, D)


def kernel(x, p, w_in, w_out, g_mix, g_ffn, conv_glu_b, conv_dw_w, conv_dw_b, conv_ln_g, conv_ln_b, w_router,
           b_router, w_up, b_up, w_down, b_down, w_ple_proj, w_ple_gate, g_ple, g_final):
    assert p.shape[0] == 1, "single trunk layer"
    return _layer(x, p[0], w_in[0], w_out[0], g_mix[0], g_ffn[0], conv_glu_b[0], conv_dw_w[0], conv_dw_b[0],
                  conv_ln_g[0], conv_ln_b[0], w_router[0], b_router[0], w_up[0], b_up[0], w_down[0], b_down[0],
                  w_ple_proj[0], w_ple_gate[0], g_ple[0], g_final)
```

```python
import functools

import jax
import jax.numpy as jnp
import numpy as np
from jax import lax
from jax.experimental import pallas as pl
from jax.experimental.pallas import tpu as pltpu

F32 = jnp.float32
BF16 = jnp.bfloat16
I32 = jnp.int32

ATTN_HEADS = 8
HEAD_DIM = 64
IDX_HEADS = 4
IDX_DIM = 64
TOPK_MAX = 256
CONV_KERNEL = 31
N_EXPERTS = 32
TOP_K = 4
SWIGLU_LIMIT = 7.0
SWIGLU_ALPHA = 1.702
ROPE_THETA = 10000.0
EPS = 1e-6

LANES = 128
SUBLANES = 8
CONV_HALO = 32
NEG_BIG = -1e30
INT_MIN = -2147483648
COUNT_CHAINS = 8
LOG2_E = 1.4426950408889634
HEADS_PER_GROUP = 8
VMEM_LIMIT = 56 * 1024 * 1024


def _params(n_axes):
    return pltpu.CompilerParams(dimension_semantics=("arbitrary",) * n_axes,
                                vmem_limit_bytes=VMEM_LIMIT)


def _rms(x, g):
    return x * lax.rsqrt(jnp.mean(x * x, axis=-1, keepdims=True) + EPS) * g


def _pack_bf16_pairs(x):
    c = x.shape[1] // 2
    lo = pltpu.bitcast(x[:, :c].astype(BF16).astype(F32), I32)
    hi = pltpu.bitcast(x[:, c:].astype(BF16).astype(F32), I32)
    return lax.shift_right_logical(lo, 16) | (hi & jnp.int32(-65536))


def _unpack_bf16_pairs(u):
    lo = pltpu.bitcast(lax.shift_left(u, 16), F32)
    hi = pltpu.bitcast(u & jnp.int32(-65536), F32)
    return jnp.concatenate([lo, hi], axis=1)


def _dot_t(a, b):
    return lax.dot_general(a, b, (((1,), (1,)), ((), ())), preferred_element_type=F32)


def _inproj_kernel(x_ref, g_ref, w_ref, wwi_ref, wvt_ref, cos_ref, sin_ref, glub_ref, dww_ref, dwb_ref, lng_ref,
                   lnb_ref, q_ref, qi_ref, k_ref, ki_ref, vt_ref, wit_ref, conv_ref, hbuf, *, tm, conv_w, row_blk):
    t = pl.program_id(1)
    xn = _rms(x_ref[...], g_ref[...]).astype(BF16)
    u = jnp.dot(xn, w_ref[...], preferred_element_type=F32)
    rope_w = (ATTN_HEADS + IDX_HEADS + 2) * HEAD_DIM
    cos = cos_ref[...]
    sin = sin_ref[...]
    r = [u[:, c * LANES:(c + 1) * LANES] * cos + u[:, rope_w + c * LANES:rope_w + (c + 1) * LANES] * sin
         for c in range(rope_w // LANES)]
    for h in range(ATTN_HEADS):
        blk = r[h // 2][:, (h % 2) * HEAD_DIM:(h % 2 + 1) * HEAD_DIM]
        q_ref[h] = (blk * (HEAD_DIM ** -0.5 * LOG2_E)).astype(BF16)
    for h in range(IDX_HEADS):
        blk = r[ATTN_HEADS // 2 + h // 2][:, (h % 2) * IDX_DIM:(h % 2 + 1) * IDX_DIM]
        qi_ref[h] = blk.astype(BF16)
    last = r[(ATTN_HEADS + IDX_HEADS) // 2]
    k_ref[...] = last[:, :HEAD_DIM].astype(BF16)
    ki_ref[...] = last[:, HEAD_DIM:].astype(BF16)
    vt = _dot_t(wvt_ref[...], xn)
    vrow = lax.broadcasted_iota(I32, vt.shape, 0)
    vt_ref[...] = jnp.where(vrow == HEAD_DIM, 1.0, vt).astype(BF16)
    wit_ref[...] = _dot_t(wwi_ref[...], xn)

    cu = u[:, 2 * rope_w:] + glub_ref[...]
    hh = cu[:, :conv_w] * jax.nn.sigmoid(cu[:, conv_w:])

    @pl.when(t == 0)
    def _():
        hbuf[0:CONV_HALO, :] = jnp.zeros((CONV_HALO, conv_w), F32)

    hbuf[CONV_HALO:CONV_HALO + tm, :] = hh
    base = CONV_HALO - (CONV_KERNEL - 1)
    for rb in range(tm // row_blk):
        acc = jnp.zeros((row_blk, conv_w), F32)
        for j in range(CONV_KERNEL):
            s0 = rb * row_blk + base + j
            acc = acc + hbuf[s0:s0 + row_blk, :] * dww_ref[j:j + 1, :]
        y = acc + dwb_ref[...]
        mu = jnp.mean(y, axis=-1, keepdims=True)
        yc = y - mu
        var = jnp.mean(yc * yc, axis=-1, keepdims=True)
        z = yc * lax.rsqrt(var + EPS) * lng_ref[...] + lnb_ref[...]
        conv_ref[rb * row_blk:(rb + 1) * row_blk, :] = (z * jax.nn.sigmoid(z)).astype(BF16)
    hbuf[0:CONV_HALO, :] = hbuf[tm:tm + CONV_HALO, :]


def _inproj(x, g_mix, w_all, w_wi_t, w_v_t, cos_t, sin_t, glu_b, dw_w, dw_b, ln_g, ln_b, tm):
    B, S, D = x.shape
    conv_w = dw_w.shape[1]
    nt = S // tm
    wtot = w_all.shape[1]
    row_blk = 64 if tm % 64 == 0 else tm
    kern = functools.partial(_inproj_kernel, tm=tm, conv_w=conv_w, row_blk=row_blk)
    full = lambda shape: pl.BlockSpec(shape, lambda b, t: (0,) * len(shape))
    out_shape = (
        jax.ShapeDtypeStruct((B, ATTN_HEADS, S, HEAD_DIM), BF16),
        jax.ShapeDtypeStruct((B, IDX_HEADS, S, IDX_DIM), BF16),
        jax.ShapeDtypeStruct((B, S, HEAD_DIM), BF16),
        jax.ShapeDtypeStruct((B, S, IDX_DIM), BF16),
        jax.ShapeDtypeStruct((B, nt, LANES, tm), BF16),
        jax.ShapeDtypeStruct((B, SUBLANES, S), F32),
        jax.ShapeDtypeStruct((B, S, conv_w), BF16),
    )
    out_specs = (
        pl.BlockSpec((None, ATTN_HEADS, tm, HEAD_DIM), lambda b, t: (b, 0, t, 0)),
        pl.BlockSpec((None, IDX_HEADS, tm, IDX_DIM), lambda b, t: (b, 0, t, 0)),
        pl.BlockSpec((None, tm, HEAD_DIM), lambda b, t: (b, t, 0)),
        pl.BlockSpec((None, tm, IDX_DIM), lambda b, t: (b, t, 0)),
        pl.BlockSpec((None, None, LANES, tm), lambda b, t: (b, t, 0, 0)),
        pl.BlockSpec((None, SUBLANES, tm), lambda b, t: (b, 0, t)),
        pl.BlockSpec((None, tm, conv_w), lambda b, t: (b, t, 0)),
    )
    in_specs = [
        pl.BlockSpec((None, tm, D), lambda b, t: (b, t, 0)),
        full((1, D)),
        full((D, wtot)),
        full((SUBLANES, D)),
        full((LANES, D)),
        pl.BlockSpec((tm, LANES), lambda b, t: (t, 0)),
        pl.BlockSpec((tm, LANES), lambda b, t: (t, 0)),
        full((1, 2 * conv_w)),
        full((CONV_KERNEL, conv_w)),
        full((1, conv_w)),
        full((1, conv_w)),
        full((1, conv_w)),
    ]
    return pl.pallas_call(
        kern, grid=(B, nt), in_specs=in_specs, out_specs=out_specs, out_shape=out_shape,
        scratch_shapes=[pltpu.VMEM((CONV_HALO + tm, conv_w), F32)],
        compiler_params=_params(2), name="inproj_rope_conv",
    )(x, g_mix, w_all, w_wi_t, w_v_t, cos_t, sin_t, glu_b, dw_w, dw_b, ln_g, ln_b)


def _count(pred, key_scr, n_chunks, qb, sc):
    lanes_acc = COUNT_CHAINS * SUBLANES
    acc = jnp.zeros((lanes_acc, qb), F32)
    for c in range(n_chunks):
        keys3 = key_scr[c * sc:(c + 1) * sc, :].reshape(sc // SUBLANES, SUBLANES, qb)
        ones = jnp.where(pred(keys3, c * sc), 1.0, 0.0).reshape(sc // lanes_acc, lanes_acc, qb)
        acc = acc + jnp.sum(ones, axis=0)
    acc = jnp.sum(acc.reshape(COUNT_CHAINS, SUBLANES, qb), axis=0)
    return jnp.broadcast_to(jnp.sum(acc, axis=0, keepdims=True), (SUBLANES, qb))


def _select(key_scr, n_chunks, k_in_chunk3, *, qb, sc, n_sel, idx_bits, seq):
    n_sel_f = jnp.float32(n_sel)
    count = functools.partial(_count, key_scr=key_scr, n_chunks=n_chunks, qb=qb, sc=sc)

    cnt0 = count(lambda k3, off: k3 >= 0)
    thr = jnp.where(cnt0 >= n_sel_f, jnp.int32(0), jnp.int32(INT_MIN))

    def bit_body(i, thr):
        cand = thr | lax.shift_left(jnp.int32(1), jnp.int32(30) - i)
        cnt = count(lambda k3, off: k3 >= cand[None])
        return jnp.where(cnt >= n_sel_f, cand, thr)

    thr = lax.fori_loop(0, 31, bit_body, thr)

    cnt_ge = count(lambda k3, off: k3 >= thr[None])

    def tie_break():
        need = n_sel_f - count(lambda k3, off: k3 > thr[None])

        def idx_body(i, jj):
            cand = jj | lax.shift_left(jnp.int32(1), jnp.int32(idx_bits - 1) - i)
            cnt = count(lambda k3, off: jnp.logical_and(k3 == thr[None], off + k_in_chunk3 < cand[None]))
            return jnp.where(cnt < need, cand, jj)

        return lax.fori_loop(0, idx_bits, idx_body, jnp.zeros((SUBLANES, qb), I32))

    jmax = lax.cond(jnp.max(cnt_ge) > n_sel_f, tie_break, lambda: jnp.full((SUBLANES, qb), seq, I32))
    return thr, jmax


def _dsa_kernel(q_ref, qi_ref, wit_ref, k_ref, ki_ref, vt_ref, o_ref, key_scr, sel_scr, m_scr, acc_scr,
                *, qb, sc, n_sel, idx_bits, seq):
    j = pl.program_id(1)
    q_lo = j * qb
    nk = (q_lo + qb + sc - 1) // sc
    q_pos = q_lo + lax.broadcasted_iota(I32, (sc, qb), 1)
    k_in_chunk = lax.broadcasted_iota(I32, (sc, qb), 0)
    k_in_chunk3 = k_in_chunk.reshape(sc // SUBLANES, SUBLANES, qb)
    q_pos3 = q_pos.reshape(sc // SUBLANES, SUBLANES, qb)

    w_eff = wit_ref[...] * (IDX_DIM ** -0.5 * IDX_HEADS ** -0.5)

    def score_body(c, carry):
        off = pl.multiple_of(c * sc, sc)
        kc = ki_ref[pl.ds(off, sc), :]
        sco = jnp.zeros((sc, qb), F32)
        for h in range(IDX_HEADS):
            sco = sco + jnp.maximum(_dot_t(kc, qi_ref[h]), 0.0) * w_eff[h:h + 1, :]
        sco = jnp.where(sco == 0.0, 0.0, sco)
        sco = jnp.where(off + k_in_chunk <= q_pos, sco, -jnp.inf)
        bits = pltpu.bitcast(sco, I32)
        key_scr[pl.ds(off, sc), :] = jnp.where(bits < 0, bits ^ jnp.int32(0x7FFFFFFF), bits)
        return carry

    lax.fori_loop(0, nk, score_body, 0)

    for n_chunks in range(1, seq // sc + 1):
        @pl.when(nk == n_chunks)
        def _(n_chunks=n_chunks):
            t, jm = _select(key_scr, n_chunks, k_in_chunk3, qb=qb, sc=sc, n_sel=n_sel, idx_bits=idx_bits, seq=seq)
            sel_scr[0] = t
            sel_scr[1] = jm

    thr = sel_scr[0]
    jmax = sel_scr[1]

    n_pairs = ATTN_HEADS // HEADS_PER_GROUP
    pw = HEADS_PER_GROUP * qb
    m_scr[...] = jnp.full(m_scr.shape, NEG_BIG, F32)
    acc_scr[...] = jnp.zeros(acc_scr.shape, F32)

    def attn_body(c, carry):
        off = pl.multiple_of(c * sc, sc)
        kc = k_ref[pl.ds(off, sc), :]
        vt = vt_ref[c]
        keys3 = key_scr[pl.ds(off, sc), :].reshape(sc // SUBLANES, SUBLANES, qb)
        kidx3 = off + k_in_chunk3
        tie = jnp.logical_and(keys3 == thr[None], kidx3 <= jmax[None])
        sel = jnp.logical_and(jnp.logical_or(keys3 > thr[None], tie), kidx3 <= q_pos3)
        bias = jnp.where(sel, 0.0, NEG_BIG)
        bias2 = jnp.concatenate([bias] * HEADS_PER_GROUP, axis=-1)
        for hp in range(n_pairs):
            qp = q_ref[HEADS_PER_GROUP * hp:HEADS_PER_GROUP * (hp + 1)].reshape(pw, HEAD_DIM)
            s3 = _dot_t(kc, qp).reshape(sc // SUBLANES, SUBLANES, pw) + bias2
            mx = jnp.max(s3, axis=0)
            mx = jnp.broadcast_to(jnp.max(mx, axis=0, keepdims=True), mx.shape)
            m_old = m_scr[hp]
            m_new = jnp.maximum(m_old, mx)
            alpha = jnp.exp2(m_old - m_new)
            p = jnp.exp2(s3 - m_new[None]).reshape(sc, pw).astype(BF16)
            pv = jnp.dot(vt, p, preferred_element_type=F32)
            acc3 = acc_scr[hp].reshape(LANES // SUBLANES, SUBLANES, pw)
            acc_scr[hp] = (acc3 * alpha[None]).reshape(LANES, pw) + pv
            m_scr[hp] = m_new
        return carry

    lax.fori_loop(0, nk, attn_body, 0)
    for hp in range(n_pairs):
        acc = acc_scr[hp]
        out_t = acc[:HEAD_DIM, :] / acc[HEAD_DIM:HEAD_DIM + 1, :]
        o_ref[HEADS_PER_GROUP * hp:HEADS_PER_GROUP * (hp + 1)] = (
            out_t.T.reshape(HEADS_PER_GROUP, qb, HEAD_DIM).astype(BF16))


def _dsa(q, qi, wit, k, ki, vt, qb, sc):
    B, H, S, dh = q.shape
    n_sel = min(TOPK_MAX, S // 4)
    idx_bits = max(1, int(np.ceil(np.log2(S))))
    assert S % sc == 0 and sc % qb == 0 and qb % LANES == 0 and vt.shape[1:] == (S // sc, LANES, sc)
    kern = functools.partial(_dsa_kernel, qb=qb, sc=sc, n_sel=n_sel, idx_bits=idx_bits, seq=S)
    in_specs = [
        pl.BlockSpec((None, H, qb, dh), lambda b, j: (b, 0, j, 0)),
        pl.BlockSpec((None, IDX_HEADS, qb, IDX_DIM), lambda b, j: (b, 0, j, 0)),
        pl.BlockSpec((None, SUBLANES, qb), lambda b, j: (b, 0, j)),
        pl.BlockSpec((None, S, dh), lambda b, j: (b, 0, 0)),
        pl.BlockSpec((None, S, IDX_DIM), lambda b, j: (b, 0, 0)),
        pl.BlockSpec((None, S // sc, LANES, sc), lambda b, j: (b, 0, 0, 0)),
    ]
    return pl.pallas_call(
        kern, grid=(B, S // qb), in_specs=in_specs,
        out_specs=pl.BlockSpec((None, H, qb, dh), lambda b, j: (b, 0, j, 0)),
        out_shape=jax.ShapeDtypeStruct((B, H, S, dh), BF16),
        scratch_shapes=[
            pltpu.VMEM((S, qb), I32),
            pltpu.VMEM((2, SUBLANES, qb), I32),
            pltpu.VMEM((H // HEADS_PER_GROUP, SUBLANES, HEADS_PER_GROUP * qb), F32),
            pltpu.VMEM((H // HEADS_PER_GROUP, LANES, HEADS_PER_GROUP * qb), F32),
        ],
        compiler_params=_params(2), name="dsa_attention",
    )(q, qi, wit, k, ki, vt)


def _mix_router_kernel(x_ref, attn_ref, conv_ref, wout_ref, g_ref, wr_ref, br_ref,
                       h1_ref, xn_ref, route_ref, gate_ref, cnt_ref, carry, *, tm):
    i = pl.program_id(0)

    @pl.when(i == 0)
    def _():
        carry[...] = jnp.zeros(carry.shape, F32)

    attn_w = ATTN_HEADS * HEAD_DIM
    acc = x_ref[...] + jnp.dot(conv_ref[...], wout_ref[attn_w:, :], preferred_element_type=F32)
    for h in range(ATTN_HEADS):
        acc = acc + jnp.dot(attn_ref[h], wout_ref[h * HEAD_DIM:(h + 1) * HEAD_DIM, :],
                            preferred_element_type=F32)
    h1_ref[...] = acc
    xn = _rms(acc, g_ref[...])
    xn_ref[...] = _pack_bf16_pairs(xn)
    logits = jnp.dot(xn.astype(BF16), wr_ref[...], preferred_element_type=F32) + br_ref[...]

    ne = logits.shape[1]
    eid = lax.broadcasted_iota(I32, (tm, ne), 1)
    work = logits
    vals, hots = [], []
    for _ in range(TOP_K):
        m = jnp.max(work, axis=-1, keepdims=True)
        e = jnp.min(jnp.where(work == m, eid, ne), axis=-1, keepdims=True)
        hot = eid == e
        work = jnp.where(hot, -jnp.inf, work)
        vals.append(m)
        hots.append(hot)
    ex = [jnp.exp(v - vals[0]) for v in vals]
    den = ex[0] + ex[1] + ex[2] + ex[3]

    multi = jnp.zeros((tm, ne), F32)
    for hot in hots:
        multi = multi + jnp.where(hot, 1.0, 0.0)
    ri = lax.broadcasted_iota(I32, (tm, tm), 0)
    ci = lax.broadcasted_iota(I32, (tm, tm), 1)
    tri = jnp.where(ci < ri, 1.0, 0.0).astype(BF16)
    before = jnp.dot(tri, multi.astype(BF16), preferred_element_type=F32) + carry[0:1, :ne]
    carry[0:1, :ne] = carry[0:1, :ne] + jnp.sum(multi, axis=0, keepdims=True)
    cnt_ref[...] = carry[...]

    lane8 = lax.broadcasted_iota(I32, (tm, 2 * TOP_K), 1)
    lane4 = lax.broadcasted_iota(I32, (tm, TOP_K), 1)
    route = jnp.zeros((tm, 2 * TOP_K), I32)
    gates = jnp.zeros((tm, TOP_K), F32)
    for kk in range(TOP_K):
        e = jnp.sum(jnp.where(hots[kk], eid, 0), axis=-1, keepdims=True)
        rk = jnp.sum(jnp.where(hots[kk], before, 0.0), axis=-1, keepdims=True).astype(I32)
        route = route + jnp.where(lane8 == kk, e, 0) + jnp.where(lane8 == TOP_K + kk, rk, 0)
        gates = gates + jnp.where(lane4 == kk, ex[kk] / den, 0.0)
    route_ref[...] = route
    gate_ref[...] = gates


def _mix_router(x2, attn, conv, w_out, g_ffn, w_router, b_router, tm):
    N, D = x2.shape
    B, H, S, dh = attn.shape
    nt = S // tm
    ne = w_router.shape[1]
    cw = conv.shape[-1]
    kern = functools.partial(_mix_router_kernel, tm=tm)
    full = lambda shape: pl.BlockSpec(shape, lambda i: (0,) * len(shape))
    in_specs = [
        pl.BlockSpec((tm, D), lambda i: (i, 0)),
        pl.BlockSpec((None, H, tm, dh), lambda i: (i // nt, 0, i % nt, 0)),
        pl.BlockSpec((None, tm, cw), lambda i: (i // nt, i % nt, 0)),
        full(w_out.shape), full((1, D)), full((D, ne)), full((1, ne)),
    ]
    out_shape = (
        jax.ShapeDtypeStruct((N, D), F32),
        jax.ShapeDtypeStruct((N, D // 2), I32),
        jax.ShapeDtypeStruct((N, 2 * TOP_K), I32),
        jax.ShapeDtypeStruct((N, TOP_K), F32),
        jax.ShapeDtypeStruct((SUBLANES, LANES), F32),
    )
    out_specs = (
        pl.BlockSpec((tm, D), lambda i: (i, 0)),
        pl.BlockSpec((tm, D // 2), lambda i: (i, 0)),
        pl.BlockSpec((tm, 2 * TOP_K), lambda i: (i, 0)),
        pl.BlockSpec((tm, TOP_K), lambda i: (i, 0)),
        pl.BlockSpec((SUBLANES, LANES), lambda i: (0, 0)),
    )
    return pl.pallas_call(
        kern, grid=(N // tm,), in_specs=in_specs, out_specs=out_specs, out_shape=out_shape,
        scratch_shapes=[pltpu.VMEM((SUBLANES, LANES), F32)],
        compiler_params=_params(1), name="outproj_router",
    )(x2, attn, conv, w_out, g_ffn, w_router, b_router)


def _row_copy(src_ref, src_row, dst_ref, dst_row, sem):
    return pltpu.make_async_copy(src_ref.at[pl.ds(src_row, 1)], dst_ref.at[pl.ds(dst_row, 1)], sem)


def _dispatch_kernel(zstart_ref, dest_ref, x_ref, out_ref, zbuf, sem, zsem, *, tb, tmx, n_blocks):
    zspan = zbuf.shape[0]

    @pl.when(pl.program_id(0) == 0)
    def _():
        zbuf[...] = jnp.zeros(zbuf.shape, zbuf.dtype)
        for e in range(N_EXPERTS):
            z0 = pl.multiple_of(zstart_ref[e], SUBLANES)
            pltpu.make_async_copy(zbuf, out_ref.at[pl.ds(z0, zspan)], zsem).start()
        for e in range(N_EXPERTS):
            pltpu.make_async_copy(zbuf, out_ref.at[pl.ds(0, zspan)], zsem).wait()

        def tail_copy(i):
            return pltpu.make_async_copy(zbuf.at[pl.ds(0, tmx)], out_ref.at[pl.ds(pl.multiple_of(i * tmx, tmx), tmx)],
                                         zsem)

        def tail_start(i, c):
            tail_copy(i).start()
            return c

        def tail_wait(i, c):
            tail_copy(i).wait()
            return c

        lax.fori_loop(zstart_ref[N_EXPERTS], n_blocks, tail_start, 0)
        lax.fori_loop(zstart_ref[N_EXPERTS], n_blocks, tail_wait, 0)

    def start(r, c):
        for kk in range(TOP_K):
            _row_copy(x_ref, r, out_ref, dest_ref[r * TOP_K + kk], sem).start()
        return c

    lax.fori_loop(0, tb, start, 0)

    def wait(r, c):
        for kk in range(TOP_K):
            _row_copy(x_ref, 0, out_ref, 0, sem).wait()
        return c

    lax.fori_loop(0, tb, wait, 0)


def _dispatch(zstart, dest_flat, xn, n_rows, tb, tmx, zspan):
    N, D = xn.shape
    kern = functools.partial(_dispatch_kernel, tb=tb, tmx=tmx, n_blocks=n_rows // tmx)
    grid_spec = pltpu.PrefetchScalarGridSpec(
        num_scalar_prefetch=1, grid=(N // tb,),
        in_specs=[
            pl.BlockSpec((tb * TOP_K,), lambda i, zs: (i,), memory_space=pltpu.SMEM),
            pl.BlockSpec((tb, D), lambda i, zs: (i, 0)),
        ],
        out_specs=pl.BlockSpec(memory_space=pl.ANY),
        scratch_shapes=[pltpu.VMEM((zspan, D), xn.dtype), pltpu.SemaphoreType.DMA(()),
                        pltpu.SemaphoreType.DMA(())],
    )
    return pl.pallas_call(
        kern, grid_spec=grid_spec,
        out_shape=jax.ShapeDtypeStruct((n_rows, D), xn.dtype),
        compiler_params=_params(1), name="moe_dispatch",
    )(zstart, dest_flat, xn)


def _expert_kernel(bexp_ref, nused_ref, x_ref, wu_ref, bu_ref, wd_ref, bd_ref, y_ref, wu_bf, wd_bf):
    i = pl.program_id(0)
    used = i < nused_ref[0]

    @pl.when(jnp.logical_and(used, jnp.logical_or(i == 0, bexp_ref[i] != bexp_ref[jnp.maximum(i - 1, 0)])))
    def _():
        wu_bf[...] = wu_ref[...].astype(BF16)
        wd_bf[...] = wd_ref[...].astype(BF16)

    @pl.when(used)
    def _():
        de = wd_ref.shape[0]
        x = _unpack_bf16_pairs(x_ref[...]).astype(BF16)
        h = jnp.dot(x, wu_bf[...], preferred_element_type=F32) + bu_ref[...]
        gt = jnp.minimum(h[:, :de], SWIGLU_LIMIT)
        lin = jnp.clip(h[:, de:], -SWIGLU_LIMIT, SWIGLU_LIMIT)
        act = gt * jax.nn.sigmoid(SWIGLU_ALPHA * gt) * (lin + 1.0)
        y = jnp.dot(act.astype(BF16), wd_bf[...], preferred_element_type=F32) + bd_ref[...]
        y_ref[...] = _pack_bf16_pairs(y)

    @pl.when(i >= nused_ref[0])
    def _():
        y_ref[...] = jnp.zeros(y_ref.shape, y_ref.dtype)


def _experts(block_exp, n_used, xs, w_up, b_up, w_down, b_down, tmx):
    P, dp = xs.shape
    E, D, two_de = w_up.shape
    de = two_de // 2
    grid_spec = pltpu.PrefetchScalarGridSpec(
        num_scalar_prefetch=2, grid=(P // tmx,),
        in_specs=[
            pl.BlockSpec((tmx, dp), lambda i, be, nu: (jnp.minimum(i, nu[0] - 1), 0)),
            pl.BlockSpec((None, D, two_de), lambda i, be, nu: (be[i], 0, 0)),
            pl.BlockSpec((None, 1, two_de), lambda i, be, nu: (be[i], 0, 0)),
            pl.BlockSpec((None, de, D), lambda i, be, nu: (be[i], 0, 0)),
            pl.BlockSpec((None, 1, D), lambda i, be, nu: (be[i], 0, 0)),
        ],
        out_specs=pl.BlockSpec((tmx, dp), lambda i, be, nu: (i, 0)),
        scratch_shapes=[pltpu.VMEM((D, two_de), BF16), pltpu.VMEM((de, D), BF16)],
    )
    return pl.pallas_call(
        _expert_kernel, grid_spec=grid_spec,
        out_shape=jax.ShapeDtypeStruct((P, dp), I32),
        compiler_params=_params(1), name="moe_experts",
    )(block_exp, n_used, xs, w_up, b_up.reshape(E, 1, two_de), w_down, b_down.reshape(E, 1, D))


def _combine_kernel(dest_ref, h1_ref, gate_ref, p_ref, ys_ref, gple_ref, wg_ref, wp_ref, gfin_ref,
                    o_ref, gbuf, sem, *, tb):
    def start(r, c):
        for kk in range(TOP_K):
            _row_copy(ys_ref, dest_ref[r * TOP_K + kk], gbuf.at[kk], r, sem).start()
        return c

    lax.fori_loop(0, tb, start, 0)

    def wait(r, c):
        for kk in range(TOP_K):
            _row_copy(ys_ref, 0, gbuf.at[kk], 0, sem).wait()
        return c

    lax.fori_loop(0, tb, wait, 0)

    g = gate_ref[...]
    h2 = h1_ref[...]
    for kk in range(TOP_K):
        h2 = h2 + g[:, kk:kk + 1] * _unpack_bf16_pairs(gbuf[kk])
    z = jnp.dot(_rms(h2, gple_ref[...]).astype(BF16), wg_ref[...], preferred_element_type=F32)
    pe = jnp.dot(p_ref[...].astype(BF16), wp_ref[...], preferred_element_type=F32)
    h3 = h2 + pe * jax.nn.sigmoid(z)
    o_ref[...] = _rms(h3, gfin_ref[...])


def _combine(dest_flat, h1, gates, p2, ys, g_ple, w_gate, w_proj, g_final, tb):
    N, D = h1.shape
    pd = p2.shape[1]
    kern = functools.partial(_combine_kernel, tb=tb)
    full = lambda shape: pl.BlockSpec(shape, lambda i: (0,) * len(shape))
    return pl.pallas_call(
        kern, grid=(N // tb,),
        in_specs=[
            pl.BlockSpec((tb * TOP_K,), lambda i: (i,), memory_space=pltpu.SMEM),
            pl.BlockSpec((tb, D), lambda i: (i, 0)),
            pl.BlockSpec((tb, TOP_K), lambda i: (i, 0)),
            pl.BlockSpec((tb, pd), lambda i: (i, 0)),
            pl.BlockSpec(memory_space=pl.ANY),
            full((1, D)), full((D, D)), full((pd, D)), full((1, D)),
        ],
        out_specs=pl.BlockSpec((tb, D), lambda i: (i, 0)),
        out_shape=jax.ShapeDtypeStruct((N, D), F32),
        scratch_shapes=[pltpu.VMEM((TOP_K, tb, ys.shape[1]), ys.dtype), pltpu.SemaphoreType.DMA(())],
        compiler_params=_params(1), name="combine_ple_norm",
    )(dest_flat, h1, gates, p2, ys, g_ple, w_gate, w_proj, g_final)


def _pick(n, prefs):
    for t in prefs:
        if n % t == 0:
            return t
    return n


def _rope_tables(S):
    inv = ROPE_THETA ** (-jnp.arange(0, HEAD_DIM, 2, dtype=F32) / HEAD_DIM)
    ang = jnp.arange(S, dtype=F32)[:, None] * inv[None, :]
    cos = jnp.cos(ang)
    sin = jnp.sin(ang)
    reps = LANES // HEAD_DIM
    return jnp.tile(jnp.concatenate([cos, cos], -1), (1, reps)), jnp.tile(jnp.concatenate([sin, sin], -1), (1, reps))


def _rot_cols(w, n_heads):
    D = w.shape[0]
    w3 = w.reshape(D, n_heads, 2, HEAD_DIM // 2)
    return jnp.stack([-w3[:, :, 1], w3[:, :, 0]], axis=2).reshape(D, n_heads * HEAD_DIM)


def _layer(h, p, w_in, w_out, g_mix, g_ffn, glu_b, dw_w, dw_b, ln_g, ln_b, w_router, b_router,
           w_up, b_up, w_down, b_down, w_ple_proj, w_ple_gate, g_ple, g_final):
    B, S, D = h.shape
    N = B * S
    aw = ATTN_HEADS * HEAD_DIM
    iw = IDX_HEADS * IDX_DIM
    o = np.cumsum([0, aw, HEAD_DIM, HEAD_DIM, iw, IDX_DIM, IDX_HEADS])
    wq, wk, wv, wqi, wki, wwi, wcu = (w_in[:, o[0]:o[1]], w_in[:, o[1]:o[2]], w_in[:, o[2]:o[3]],
                                      w_in[:, o[3]:o[4]], w_in[:, o[4]:o[5]], w_in[:, o[5]:o[6]], w_in[:, o[6]:])
    rope_w = jnp.concatenate([wq, wqi, wk, wki], axis=1)
    rope_rot = jnp.concatenate([_rot_cols(wq, ATTN_HEADS), _rot_cols(wqi, IDX_HEADS),
                                _rot_cols(wk, 1), _rot_cols(wki, 1)], axis=1)
    w_all = jnp.concatenate([rope_w, rope_rot, wcu], axis=1).astype(BF16)
    w_wi_t = jnp.concatenate([wwi.T, jnp.zeros((SUBLANES - IDX_HEADS, D), F32)], axis=0).astype(BF16)
    w_v_t = jnp.concatenate([wv.T, jnp.zeros((LANES - HEAD_DIM, D), F32)], axis=0).astype(BF16)
    cos_t, sin_t = _rope_tables(S)

    tm = _pick(S, (512, 256, 128))
    q, qi, k, ki, vt, wit, conv = _inproj(h, g_mix[None], w_all, w_wi_t, w_v_t, cos_t, sin_t, glu_b[None], dw_w,
                                          dw_b[None], ln_g[None], ln_b[None], tm)
    qb = _pick(S, (128,))
    attn = _dsa(q, qi, wit, k, ki, vt, qb, tm)

    h1, xn, route, gates, cnt = _mix_router(h.reshape(N, D), attn, conv, w_out.astype(BF16), g_ffn[None],
                                            w_router.astype(BF16), b_router[None], tm)

    tmx = 512
    counts = cnt[0, :N_EXPERTS].astype(I32)
    padded = (counts + tmx - 1) // tmx * tmx
    cum = jnp.cumsum(padded)
    pstart = cum - padded
    n_blocks = (N * TOP_K + N_EXPERTS * (tmx - 1) + tmx - 1) // tmx
    n_rows = n_blocks * tmx
    dest = (pstart[route[:, :TOP_K]] + route[:, TOP_K:]).reshape(N * TOP_K)
    n_used = (cum[-1] // tmx).astype(I32)
    blk_start = jnp.minimum(jnp.arange(n_blocks, dtype=I32), n_used - 1) * tmx
    block_exp = jnp.minimum(jnp.sum((blk_start[:, None] >= cum[None, :]).astype(I32), axis=1), N_EXPERTS - 1)
    zspan = tmx + SUBLANES
    zstart = (jnp.minimum(pstart + counts, n_rows - zspan) // SUBLANES * SUBLANES).astype(I32)
    zstart = jnp.concatenate([zstart, n_used.reshape(1)])

    tb = _pick(N, (256,))
    xs = _dispatch(zstart, dest, xn, n_rows, tb, tmx, zspan)
    ys = _experts(block_exp, n_used.reshape(1), xs, w_up, b_up, w_down, b_down, tmx)
    out = _combine(dest, h1, gates, p.reshape(N, -1), ys, g_ple[None], w_ple_gate.astype(BF16),
                   w_ple_proj.astype(BF16), g_final[None], tb)
    return out.reshape(B, S, D)


def kernel(x, p, w_in, w_out, g_mix, g_ffn, conv_glu_b, conv_dw_w, conv_dw_b, conv_ln_g, conv_ln_b, w_router,
           b_router, w_up, b_up, w_down, b_down, w_ple_proj, w_ple_gate, g_ple, g_final):
    assert p.shape[0] == 1, "single trunk layer"
    return _layer(x, p[0], w_in[0], w_out[0], g_mix[0], g_ffn[0], conv_glu_b[0], conv_dw_w[0], conv_dw_b[0],
                  conv_ln_g[0], conv_ln_b[0], w_router[0], b_router[0], w_up[0], b_up[0], w_down[0], b_down[0],
                  w_ple_proj[0], w_ple_gate[0], g_ple[0], g_final)
```

```python
import functools

import jax
import jax.numpy as jnp
import numpy as np
from jax import lax
from jax.experimental import pallas as pl
from jax.experimental.pallas import tpu as pltpu

F32 = jnp.float32
BF16 = jnp.bfloat16
I32 = jnp.int32
I16 = jnp.int16

ATTN_HEADS = 8
HEAD_DIM = 64
IDX_HEADS = 4
IDX_DIM = 64
TOPK_MAX = 256
CONV_KERNEL = 31
N_EXPERTS = 32
TOP_K = 4
SWIGLU_LIMIT = 7.0
SWIGLU_ALPHA = 1.702
ROPE_THETA = 10000.0
EPS = 1e-6

LANES = 128
SUBLANES = 8
CONV_HALO = 32
NEG_BIG = -1e30
INT_MIN = -2147483648
COUNT_CHAINS = 8
HALF_ROWS = 16
HALF_CHAINS = 4
LOG2_E = 1.4426950408889634
HEADS_PER_GROUP = 8
VMEM_LIMIT = 56 * 1024 * 1024


def _params(n_axes):
    return pltpu.CompilerParams(dimension_semantics=("arbitrary",) * n_axes,
                                vmem_limit_bytes=VMEM_LIMIT)


def _rms(x, g):
    return x * lax.rsqrt(jnp.mean(x * x, axis=-1, keepdims=True) + EPS) * g


def _pack_bf16_pairs(x):
    c = x.shape[1] // 2
    lo = pltpu.bitcast(x[:, :c].astype(BF16).astype(F32), I32)
    hi = pltpu.bitcast(x[:, c:].astype(BF16).astype(F32), I32)
    return lax.shift_right_logical(lo, 16) | (hi & jnp.int32(-65536))


def _unpack_bf16_pairs(u):
    lo = pltpu.bitcast(lax.shift_left(u, 16), F32)
    hi = pltpu.bitcast(u & jnp.int32(-65536), F32)
    return jnp.concatenate([lo, hi], axis=1)


def _dot_t(a, b):
    return lax.dot_general(a, b, (((1,), (1,)), ((), ())), preferred_element_type=F32)


def _inproj_kernel(x_ref, g_ref, w_ref, wwi_ref, wvt_ref, cos_ref, sin_ref, glub_ref, dww_ref, dwb_ref, lng_ref,
                   lnb_ref, q_ref, qi_ref, k_ref, ki_ref, vt_ref, wit_ref, conv_ref, hbuf, *, tm, conv_w, row_blk):
    t = pl.program_id(1)
    xn = _rms(x_ref[...], g_ref[...]).astype(BF16)
    u = jnp.dot(xn, w_ref[...], preferred_element_type=F32)
    rope_w = (ATTN_HEADS + IDX_HEADS + 2) * HEAD_DIM
    cos = cos_ref[...]
    sin = sin_ref[...]
    r = [u[:, c * LANES:(c + 1) * LANES] * cos + u[:, rope_w + c * LANES:rope_w + (c + 1) * LANES] * sin
         for c in range(rope_w // LANES)]
    for h in range(ATTN_HEADS):
        blk = r[h // 2][:, (h % 2) * HEAD_DIM:(h % 2 + 1) * HEAD_DIM]
        q_ref[h] = (blk * (HEAD_DIM ** -0.5 * LOG2_E)).astype(BF16)
    for h in range(IDX_HEADS):
        blk = r[ATTN_HEADS // 2 + h // 2][:, (h % 2) * IDX_DIM:(h % 2 + 1) * IDX_DIM]
        qi_ref[h] = blk.astype(BF16)
    last = r[(ATTN_HEADS + IDX_HEADS) // 2]
    k_ref[...] = last[:, :HEAD_DIM].astype(BF16)
    ki_ref[...] = last[:, HEAD_DIM:].astype(BF16)
    vt = _dot_t(wvt_ref[...], xn)
    vrow = lax.broadcasted_iota(I32, vt.shape, 0)
    vt_ref[...] = jnp.where(vrow == HEAD_DIM, 1.0, vt).astype(BF16)
    wit_ref[...] = _dot_t(wwi_ref[...], xn)

    cu = u[:, 2 * rope_w:] + glub_ref[...]
    hh = cu[:, :conv_w] * jax.nn.sigmoid(cu[:, conv_w:])

    @pl.when(t == 0)
    def _():
        hbuf[0:CONV_HALO, :] = jnp.zeros((CONV_HALO, conv_w), F32)

    hbuf[CONV_HALO:CONV_HALO + tm, :] = hh
    base = CONV_HALO - (CONV_KERNEL - 1)
    for rb in range(tm // row_blk):
        acc = jnp.zeros((row_blk, conv_w), F32)
        for j in range(CONV_KERNEL):
            s0 = rb * row_blk + base + j
            acc = acc + hbuf[s0:s0 + row_blk, :] * dww_ref[j:j + 1, :]
        y = acc + dwb_ref[...]
        mu = jnp.mean(y, axis=-1, keepdims=True)
        yc = y - mu
        var = jnp.mean(yc * yc, axis=-1, keepdims=True)
        z = yc * lax.rsqrt(var + EPS) * lng_ref[...] + lnb_ref[...]
        conv_ref[rb * row_blk:(rb + 1) * row_blk, :] = (z * jax.nn.sigmoid(z)).astype(BF16)
    hbuf[0:CONV_HALO, :] = hbuf[tm:tm + CONV_HALO, :]


def _inproj(x, g_mix, w_all, w_wi_t, w_v_t, cos_t, sin_t, glu_b, dw_w, dw_b, ln_g, ln_b, tm):
    B, S, D = x.shape
    conv_w = dw_w.shape[1]
    nt = S // tm
    wtot = w_all.shape[1]
    row_blk = 64 if tm % 64 == 0 else tm
    kern = functools.partial(_inproj_kernel, tm=tm, conv_w=conv_w, row_blk=row_blk)
    full = lambda shape: pl.BlockSpec(shape, lambda b, t: (0,) * len(shape))
    out_shape = (
        jax.ShapeDtypeStruct((B, ATTN_HEADS, S, HEAD_DIM), BF16),
        jax.ShapeDtypeStruct((B, IDX_HEADS, S, IDX_DIM), BF16),
        jax.ShapeDtypeStruct((B, S, HEAD_DIM), BF16),
        jax.ShapeDtypeStruct((B, S, IDX_DIM), BF16),
        jax.ShapeDtypeStruct((B, nt, LANES, tm), BF16),
        jax.ShapeDtypeStruct((B, SUBLANES, S), F32),
        jax.ShapeDtypeStruct((B, S, conv_w), BF16),
    )
    out_specs = (
        pl.BlockSpec((None, ATTN_HEADS, tm, HEAD_DIM), lambda b, t: (b, 0, t, 0)),
        pl.BlockSpec((None, IDX_HEADS, tm, IDX_DIM), lambda b, t: (b, 0, t, 0)),
        pl.BlockSpec((None, tm, HEAD_DIM), lambda b, t: (b, t, 0)),
        pl.BlockSpec((None, tm, IDX_DIM), lambda b, t: (b, t, 0)),
        pl.BlockSpec((None, None, LANES, tm), lambda b, t: (b, t, 0, 0)),
        pl.BlockSpec((None, SUBLANES, tm), lambda b, t: (b, 0, t)),
        pl.BlockSpec((None, tm, conv_w), lambda b, t: (b, t, 0)),
    )
    in_specs = [
        pl.BlockSpec((None, tm, D), lambda b, t: (b, t, 0)),
        full((1, D)),
        full((D, wtot)),
        full((SUBLANES, D)),
        full((LANES, D)),
        pl.BlockSpec((tm, LANES), lambda b, t: (t, 0)),
        pl.BlockSpec((tm, LANES), lambda b, t: (t, 0)),
        full((1, 2 * conv_w)),
        full((CONV_KERNEL, conv_w)),
        full((1, conv_w)),
        full((1, conv_w)),
        full((1, conv_w)),
    ]
    return pl.pallas_call(
        kern, grid=(B, nt), in_specs=in_specs, out_specs=out_specs, out_shape=out_shape,
        scratch_shapes=[pltpu.VMEM((CONV_HALO + tm, conv_w), F32)],
        compiler_params=_params(2), name="inproj_rope_conv",
    )(x, g_mix, w_all, w_wi_t, w_v_t, cos_t, sin_t, glu_b, dw_w, dw_b, ln_g, ln_b)


def _count(pred, key_scr, n_chunks, qb, sc):
    lanes_acc = COUNT_CHAINS * SUBLANES
    acc = jnp.zeros((lanes_acc, qb), F32)
    for c in range(n_chunks):
        keys3 = key_scr[c * sc:(c + 1) * sc, :].reshape(sc // SUBLANES, SUBLANES, qb)
        ones = jnp.where(pred(keys3, c * sc), 1.0, 0.0).reshape(sc // lanes_acc, lanes_acc, qb)
        acc = acc + jnp.sum(ones, axis=0)
    acc = jnp.sum(acc.reshape(COUNT_CHAINS, SUBLANES, qb), axis=0)
    return jnp.broadcast_to(jnp.sum(acc, axis=0, keepdims=True), (SUBLANES, qb))


def _count16(pred, half_scr, n_chunks, qb, sc):
    rows = HALF_CHAINS * HALF_ROWS
    acc = jnp.zeros((rows, qb), I16)
    for c in range(n_chunks):
        h3 = half_scr[c * sc:(c + 1) * sc, :].reshape(sc // HALF_ROWS, HALF_ROWS, qb)
        ones = jnp.where(pred(h3), jnp.int16(1), jnp.int16(0)).reshape(sc // rows, rows, qb)
        for i in range(sc // rows):
            acc = acc + ones[i]
    tot = jnp.sum(acc.astype(F32), axis=0, keepdims=True)
    return jnp.broadcast_to(tot, (HALF_ROWS, qb))


def _bisect16(half_scr, n_chunks, n_sel_f, *, qb, sc):
    count = functools.partial(_count16, half_scr=half_scr, n_chunks=n_chunks, qb=qb, sc=sc)
    cnt0 = count(lambda h3: h3 >= jnp.int16(0))
    thr = jnp.where(cnt0 >= n_sel_f, jnp.int32(0), jnp.int32(-32768))

    def bit_body(i, thr):
        cand = thr | lax.shift_left(jnp.int32(1), jnp.int32(14) - i)
        cand16 = cand.astype(I16)
        cnt = count(lambda h3: h3 >= cand16[None])
        return jnp.where(cnt >= n_sel_f, cand, thr)

    return lax.fori_loop(0, 15, bit_body, thr)


def _select(key_scr, hi_scr, lo_scr, n_chunks, k_in_chunk3, *, qb, sc, n_sel, idx_bits, seq):
    n_sel_f = jnp.float32(n_sel)
    count = functools.partial(_count, key_scr=key_scr, n_chunks=n_chunks, qb=qb, sc=sc)

    t_hi = _bisect16(hi_scr, n_chunks, n_sel_f, qb=qb, sc=sc)
    t_hi16 = t_hi.astype(I16)
    for c in range(n_chunks):
        rows = slice(c * sc, (c + 1) * sc)
        hi3 = hi_scr[rows, :].reshape(sc // HALF_ROWS, HALF_ROWS, qb)
        lo3 = lo_scr[rows, :].reshape(sc // HALF_ROWS, HALF_ROWS, qb)
        other = jnp.where(hi3 > t_hi16[None], jnp.int16(32767), jnp.int16(-32768))
        lo_scr[rows, :] = jnp.where(hi3 == t_hi16[None], lo3, other).reshape(sc, qb)
    t_lo = _bisect16(lo_scr, n_chunks, n_sel_f, qb=qb, sc=sc)
    thr = (lax.shift_left(t_hi, 16) | ((t_lo + 32768) & 0xFFFF))[:SUBLANES]

    cnt_ge = count(lambda k3, off: k3 >= thr[None])

    def tie_break():
        need = n_sel_f - count(lambda k3, off: k3 > thr[None])

        def idx_body(i, jj):
            cand = jj | lax.shift_left(jnp.int32(1), jnp.int32(idx_bits - 1) - i)
            cnt = count(lambda k3, off: jnp.logical_and(k3 == thr[None], off + k_in_chunk3 < cand[None]))
            return jnp.where(cnt < need, cand, jj)

        return lax.fori_loop(0, idx_bits, idx_body, jnp.zeros((SUBLANES, qb), I32))

    jmax = lax.cond(jnp.max(cnt_ge) > n_sel_f, tie_break, lambda: jnp.full((SUBLANES, qb), seq, I32))
    return thr, jmax


def _dsa_kernel(q_ref, qi_ref, wit_ref, k_ref, ki_ref, vt_ref, o_ref, key_scr, hi_scr, lo_scr, sel_scr, m_scr, acc_scr,
                *, qb, sc, n_sel, idx_bits, seq):
    j = pl.program_id(1)
    q_lo = j * qb
    nk = (q_lo + qb + sc - 1) // sc
    q_pos = q_lo + lax.broadcasted_iota(I32, (sc, qb), 1)
    k_in_chunk = lax.broadcasted_iota(I32, (sc, qb), 0)
    k_in_chunk3 = k_in_chunk.reshape(sc // SUBLANES, SUBLANES, qb)
    q_pos3 = q_pos.reshape(sc // SUBLANES, SUBLANES, qb)

    w_eff = wit_ref[...] * (IDX_DIM ** -0.5 * IDX_HEADS ** -0.5)

    def score_body(c, carry):
        off = pl.multiple_of(c * sc, sc)
        kc = ki_ref[pl.ds(off, sc), :]
        sco = jnp.zeros((sc, qb), F32)
        for h in range(IDX_HEADS):
            sco = sco + jnp.maximum(_dot_t(kc, qi_ref[h]), 0.0) * w_eff[h:h + 1, :]
        sco = jnp.where(sco == 0.0, 0.0, sco)
        sco = jnp.where(off + k_in_chunk <= q_pos, sco, -jnp.inf)
        bits = pltpu.bitcast(sco, I32)
        keys = jnp.where(bits < 0, bits ^ jnp.int32(0x7FFFFFFF), bits)
        key_scr[pl.ds(off, sc), :] = keys
        hi_scr[pl.ds(off, sc), :] = lax.shift_right_arithmetic(keys, 16).astype(I16)
        lo_scr[pl.ds(off, sc), :] = ((keys & 0xFFFF) - 32768).astype(I16)
        return carry

    lax.fori_loop(0, nk, score_body, 0)

    for n_chunks in range(1, seq // sc + 1):
        @pl.when(nk == n_chunks)
        def _(n_chunks=n_chunks):
            t, jm = _select(key_scr, hi_scr, lo_scr, n_chunks, k_in_chunk3, qb=qb, sc=sc, n_sel=n_sel,
                            idx_bits=idx_bits, seq=seq)
            sel_scr[0] = t
            sel_scr[1] = jm

    thr = sel_scr[0]
    jmax = sel_scr[1]

    n_pairs = ATTN_HEADS // HEADS_PER_GROUP
    pw = HEADS_PER_GROUP * qb
    m_scr[...] = jnp.full(m_scr.shape, NEG_BIG, F32)
    acc_scr[...] = jnp.zeros(acc_scr.shape, F32)

    def attn_body(c, carry):
        off = pl.multiple_of(c * sc, sc)
        kc = k_ref[pl.ds(off, sc), :]
        vt = vt_ref[c]
        keys3 = key_scr[pl.ds(off, sc), :].reshape(sc // SUBLANES, SUBLANES, qb)
        kidx3 = off + k_in_chunk3
        tie = jnp.logical_and(keys3 == thr[None], kidx3 <= jmax[None])
        sel = jnp.logical_and(jnp.logical_or(keys3 > thr[None], tie), kidx3 <= q_pos3)
        bias = jnp.where(sel, 0.0, NEG_BIG)
        bias2 = jnp.concatenate([bias] * HEADS_PER_GROUP, axis=-1)
        for hp in range(n_pairs):
            qp = q_ref[HEADS_PER_GROUP * hp:HEADS_PER_GROUP * (hp + 1)].reshape(pw, HEAD_DIM)
            s3 = _dot_t(kc, qp).reshape(sc // SUBLANES, SUBLANES, pw) + bias2
            mx = jnp.max(s3, axis=0)
            mx = jnp.broadcast_to(jnp.max(mx, axis=0, keepdims=True), mx.shape)
            m_old = m_scr[hp]
            m_new = jnp.maximum(m_old, mx)
            alpha = jnp.exp2(m_old - m_new)
            p = jnp.exp2(s3 - m_new[None]).reshape(sc, pw).astype(BF16)
            pv = jnp.dot(vt, p, preferred_element_type=F32)
            acc3 = acc_scr[hp].reshape(LANES // SUBLANES, SUBLANES, pw)
            acc_scr[hp] = (acc3 * alpha[None]).reshape(LANES, pw) + pv
            m_scr[hp] = m_new
        return carry

    lax.fori_loop(0, nk, attn_body, 0)
    for hp in range(n_pairs):
        acc = acc_scr[hp]
        out_t = acc[:HEAD_DIM, :] / acc[HEAD_DIM:HEAD_DIM + 1, :]
        o_ref[HEADS_PER_GROUP * hp:HEADS_PER_GROUP * (hp + 1)] = (
            out_t.T.reshape(HEADS_PER_GROUP, qb, HEAD_DIM).astype(BF16))


def _dsa(q, qi, wit, k, ki, vt, qb, sc):
    B, H, S, dh = q.shape
    n_sel = min(TOPK_MAX, S // 4)
    idx_bits = max(1, int(np.ceil(np.log2(S))))
    assert S % sc == 0 and sc % qb == 0 and qb % LANES == 0 and vt.shape[1:] == (S // sc, LANES, sc)
    kern = functools.partial(_dsa_kernel, qb=qb, sc=sc, n_sel=n_sel, idx_bits=idx_bits, seq=S)
    in_specs = [
        pl.BlockSpec((None, H, qb, dh), lambda b, j: (b, 0, j, 0)),
        pl.BlockSpec((None, IDX_HEADS, qb, IDX_DIM), lambda b, j: (b, 0, j, 0)),
        pl.BlockSpec((None, SUBLANES, qb), lambda b, j: (b, 0, j)),
        pl.BlockSpec((None, S, dh), lambda b, j: (b, 0, 0)),
        pl.BlockSpec((None, S, IDX_DIM), lambda b, j: (b, 0, 0)),
        pl.BlockSpec((None, S // sc, LANES, sc), lambda b, j: (b, 0, 0, 0)),
    ]
    return pl.pallas_call(
        kern, grid=(B, S // qb), in_specs=in_specs,
        out_specs=pl.BlockSpec((None, H, qb, dh), lambda b, j: (b, 0, j, 0)),
        out_shape=jax.ShapeDtypeStruct((B, H, S, dh), BF16),
        scratch_shapes=[
            pltpu.VMEM((S, qb), I32),
            pltpu.VMEM((S, qb), I16),
            pltpu.VMEM((S, qb), I16),
            pltpu.VMEM((2, SUBLANES, qb), I32),
            pltpu.VMEM((H // HEADS_PER_GROUP, SUBLANES, HEADS_PER_GROUP * qb), F32),
            pltpu.VMEM((H // HEADS_PER_GROUP, LANES, HEADS_PER_GROUP * qb), F32),
        ],
        compiler_params=_params(2), name="dsa_attention",
    )(q, qi, wit, k, ki, vt)


def _mix_router_kernel(x_ref, attn_ref, conv_ref, wout_ref, g_ref, wr_ref, br_ref,
                       h1_ref, xn_ref, route_ref, gate_ref, cnt_ref, carry, *, tm):
    i = pl.program_id(0)

    @pl.when(i == 0)
    def _():
        carry[...] = jnp.zeros(carry.shape, F32)

    attn_w = ATTN_HEADS * HEAD_DIM
    acc = x_ref[...] + jnp.dot(conv_ref[...], wout_ref[attn_w:, :], preferred_element_type=F32)
    for h in range(ATTN_HEADS):
        acc = acc + jnp.dot(attn_ref[h], wout_ref[h * HEAD_DIM:(h + 1) * HEAD_DIM, :],
                            preferred_element_type=F32)
    h1_ref[...] = acc
    xn = _rms(acc, g_ref[...])
    xn_ref[...] = _pack_bf16_pairs(xn)
    logits = jnp.dot(xn.astype(BF16), wr_ref[...], preferred_element_type=F32) + br_ref[...]

    ne = logits.shape[1]
    eid = lax.broadcasted_iota(I32, (tm, ne), 1)
    work = logits
    vals, hots = [], []
    for _ in range(TOP_K):
        m = jnp.max(work, axis=-1, keepdims=True)
        e = jnp.min(jnp.where(work == m, eid, ne), axis=-1, keepdims=True)
        hot = eid == e
        work = jnp.where(hot, -jnp.inf, work)
        vals.append(m)
        hots.append(hot)
    ex = [jnp.exp(v - vals[0]) for v in vals]
    den = ex[0] + ex[1] + ex[2] + ex[3]

    multi = jnp.zeros((tm, ne), F32)
    for hot in hots:
        multi = multi + jnp.where(hot, 1.0, 0.0)
    ri = lax.broadcasted_iota(I32, (tm, tm), 0)
    ci = lax.broadcasted_iota(I32, (tm, tm), 1)
    tri = jnp.where(ci < ri, 1.0, 0.0).astype(BF16)
    before = jnp.dot(tri, multi.astype(BF16), preferred_element_type=F32) + carry[0:1, :ne]
    carry[0:1, :ne] = carry[0:1, :ne] + jnp.sum(multi, axis=0, keepdims=True)
    cnt_ref[...] = carry[...]

    lane8 = lax.broadcasted_iota(I32, (tm, 2 * TOP_K), 1)
    lane4 = lax.broadcasted_iota(I32, (tm, TOP_K), 1)
    route = jnp.zeros((tm, 2 * TOP_K), I32)
    gates = jnp.zeros((tm, TOP_K), F32)
    for kk in range(TOP_K):
        e = jnp.sum(jnp.where(hots[kk], eid, 0), axis=-1, keepdims=True)
        rk = jnp.sum(jnp.where(hots[kk], before, 0.0), axis=-1, keepdims=True).astype(I32)
        route = route + jnp.where(lane8 == kk, e, 0) + jnp.where(lane8 == TOP_K + kk, rk, 0)
        gates = gates + jnp.where(lane4 == kk, ex[kk] / den, 0.0)
    route_ref[...] = route
    gate_ref[...] = gates


def _mix_router(x2, attn, conv, w_out, g_ffn, w_router, b_router, tm):
    N, D = x2.shape
    B, H, S, dh = attn.shape
    nt = S // tm
    ne = w_router.shape[1]
    cw = conv.shape[-1]
    kern = functools.partial(_mix_router_kernel, tm=tm)
    full = lambda shape: pl.BlockSpec(shape, lambda i: (0,) * len(shape))
    in_specs = [
        pl.BlockSpec((tm, D), lambda i: (i, 0)),
        pl.BlockSpec((None, H, tm, dh), lambda i: (i // nt, 0, i % nt, 0)),
        pl.BlockSpec((None, tm, cw), lambda i: (i // nt, i % nt, 0)),
        full(w_out.shape), full((1, D)), full((D, ne)), full((1, ne)),
    ]
    out_shape = (
        jax.ShapeDtypeStruct((N, D), F32),
        jax.ShapeDtypeStruct((N, D // 2), I32),
        jax.ShapeDtypeStruct((N, 2 * TOP_K), I32),
        jax.ShapeDtypeStruct((N, TOP_K), F32),
        jax.ShapeDtypeStruct((SUBLANES, LANES), F32),
    )
    out_specs = (
        pl.BlockSpec((tm, D), lambda i: (i, 0)),
        pl.BlockSpec((tm, D // 2), lambda i: (i, 0)),
        pl.BlockSpec((tm, 2 * TOP_K), lambda i: (i, 0)),
        pl.BlockSpec((tm, TOP_K), lambda i: (i, 0)),
        pl.BlockSpec((SUBLANES, LANES), lambda i: (0, 0)),
    )
    return pl.pallas_call(
        kern, grid=(N // tm,), in_specs=in_specs, out_specs=out_specs, out_shape=out_shape,
        scratch_shapes=[pltpu.VMEM((SUBLANES, LANES), F32)],
        compiler_params=_params(1), name="outproj_router",
    )(x2, attn, conv, w_out, g_ffn, w_router, b_router)


def _row_copy(src_ref, src_row, dst_ref, dst_row, sem):
    return pltpu.make_async_copy(src_ref.at[pl.ds(src_row, 1)], dst_ref.at[pl.ds(dst_row, 1)], sem)


def _dispatch_kernel(zstart_ref, dest_ref, x_ref, out_ref, zbuf, sem, zsem, *, tb, tmx, n_blocks):
    zspan = zbuf.shape[0]

    @pl.when(pl.program_id(0) == 0)
    def _():
        zbuf[...] = jnp.zeros(zbuf.shape, zbuf.dtype)
        for e in range(N_EXPERTS):
            z0 = pl.multiple_of(zstart_ref[e], SUBLANES)
            pltpu.make_async_copy(zbuf, out_ref.at[pl.ds(z0, zspan)], zsem).start()
        for e in range(N_EXPERTS):
            pltpu.make_async_copy(zbuf, out_ref.at[pl.ds(0, zspan)], zsem).wait()

        def tail_copy(i):
            return pltpu.make_async_copy(zbuf.at[pl.ds(0, tmx)], out_ref.at[pl.ds(pl.multiple_of(i * tmx, tmx), tmx)],
                                         zsem)

        def tail_start(i, c):
            tail_copy(i).start()
            return c

        def tail_wait(i, c):
            tail_copy(i).wait()
            return c

        lax.fori_loop(zstart_ref[N_EXPERTS], n_blocks, tail_start, 0)
        lax.fori_loop(zstart_ref[N_EXPERTS], n_blocks, tail_wait, 0)

    def start(r, c):
        for kk in range(TOP_K):
            _row_copy(x_ref, r, out_ref, dest_ref[r * TOP_K + kk], sem).start()
        return c

    lax.fori_loop(0, tb, start, 0)

    def wait(r, c):
        for kk in range(TOP_K):
            _row_copy(x_ref, 0, out_ref, 0, sem).wait()
        return c

    lax.fori_loop(0, tb, wait, 0)


def _dispatch(zstart, dest_flat, xn, n_rows, tb, tmx, zspan):
    N, D = xn.shape
    kern = functools.partial(_dispatch_kernel, tb=tb, tmx=tmx, n_blocks=n_rows // tmx)
    grid_spec = pltpu.PrefetchScalarGridSpec(
        num_scalar_prefetch=1, grid=(N // tb,),
        in_specs=[
            pl.BlockSpec((tb * TOP_K,), lambda i, zs: (i,), memory_space=pltpu.SMEM),
            pl.BlockSpec((tb, D), lambda i, zs: (i, 0)),
        ],
        out_specs=pl.BlockSpec(memory_space=pl.ANY),
        scratch_shapes=[pltpu.VMEM((zspan, D), xn.dtype), pltpu.SemaphoreType.DMA(()),
                        pltpu.SemaphoreType.DMA(())],
    )
    return pl.pallas_call(
        kern, grid_spec=grid_spec,
        out_shape=jax.ShapeDtypeStruct((n_rows, D), xn.dtype),
        compiler_params=_params(1), name="moe_dispatch",
    )(zstart, dest_flat, xn)


def _expert_kernel(bexp_ref, nused_ref, x_ref, wu_ref, bu_ref, wd_ref, bd_ref, y_ref, wu_bf, wd_bf):
    i = pl.program_id(0)
    used = i < nused_ref[0]

    @pl.when(jnp.logical_and(used, jnp.logical_or(i == 0, bexp_ref[i] != bexp_ref[jnp.maximum(i - 1, 0)])))
    def _():
        wu_bf[...] = wu_ref[...].astype(BF16)
        wd_bf[...] = wd_ref[...].astype(BF16)

    @pl.when(used)
    def _():
        de = wd_ref.shape[0]
        x = _unpack_bf16_pairs(x_ref[...]).astype(BF16)
        h = jnp.dot(x, wu_bf[...], preferred_element_type=F32) + bu_ref[...]
        gt = jnp.minimum(h[:, :de], SWIGLU_LIMIT)
        lin = jnp.clip(h[:, de:], -SWIGLU_LIMIT, SWIGLU_LIMIT)
        act = gt * jax.nn.sigmoid(SWIGLU_ALPHA * gt) * (lin + 1.0)
        y = jnp.dot(act.astype(BF16), wd_bf[...], preferred_element_type=F32) + bd_ref[...]
        y_ref[...] = _pack_bf16_pairs(y)

    @pl.when(i >= nused_ref[0])
    def _():
        y_ref[...] = jnp.zeros(y_ref.shape, y_ref.dtype)


def _experts(block_exp, n_used, xs, w_up, b_up, w_down, b_down, tmx):
    P, dp = xs.shape
    E, D, two_de = w_up.shape
    de = two_de // 2
    grid_spec = pltpu.PrefetchScalarGridSpec(
        num_scalar_prefetch=2, grid=(P // tmx,),
        in_specs=[
            pl.BlockSpec((tmx, dp), lambda i, be, nu: (jnp.minimum(i, nu[0] - 1), 0)),
            pl.BlockSpec((None, D, two_de), lambda i, be, nu: (be[i], 0, 0)),
            pl.BlockSpec((None, 1, two_de), lambda i, be, nu: (be[i], 0, 0)),
            pl.BlockSpec((None, de, D), lambda i, be, nu: (be[i], 0, 0)),
            pl.BlockSpec((None, 1, D), lambda i, be, nu: (be[i], 0, 0)),
        ],
        out_specs=pl.BlockSpec((tmx, dp), lambda i, be, nu: (i, 0)),
        scratch_shapes=[pltpu.VMEM((D, two_de), BF16), pltpu.VMEM((de, D), BF16)],
    )
    return pl.pallas_call(
        _expert_kernel, grid_spec=grid_spec,
        out_shape=jax.ShapeDtypeStruct((P, dp), I32),
        compiler_params=_params(1), name="moe_experts",
    )(block_exp, n_used, xs, w_up, b_up.reshape(E, 1, two_de), w_down, b_down.reshape(E, 1, D))


def _combine_kernel(dest_ref, h1_ref, gate_ref, p_ref, ys_ref, gple_ref, wg_ref, wp_ref, gfin_ref,
                    o_ref, gbuf, sem, *, tb):
    def start(r, c):
        for kk in range(TOP_K):
            _row_copy(ys_ref, dest_ref[r * TOP_K + kk], gbuf.at[kk], r, sem).start()
        return c

    lax.fori_loop(0, tb, start, 0)

    def wait(r, c):
        for kk in range(TOP_K):
            _row_copy(ys_ref, 0, gbuf.at[kk], 0, sem).wait()
        return c

    lax.fori_loop(0, tb, wait, 0)

    g = gate_ref[...]
    h2 = h1_ref[...]
    for kk in range(TOP_K):
        h2 = h2 + g[:, kk:kk + 1] * _unpack_bf16_pairs(gbuf[kk])
    z = jnp.dot(_rms(h2, gple_ref[...]).astype(BF16), wg_ref[...], preferred_element_type=F32)
    pe = jnp.dot(p_ref[...].astype(BF16), wp_ref[...], preferred_element_type=F32)
    h3 = h2 + pe * jax.nn.sigmoid(z)
    o_ref[...] = _rms(h3, gfin_ref[...])


def _combine(dest_flat, h1, gates, p2, ys, g_ple, w_gate, w_proj, g_final, tb):
    N, D = h1.shape
    pd = p2.shape[1]
    kern = functools.partial(_combine_kernel, tb=tb)
    full = lambda shape: pl.BlockSpec(shape, lambda i: (0,) * len(shape))
    return pl.pallas_call(
        kern, grid=(N // tb,),
        in_specs=[
            pl.BlockSpec((tb * TOP_K,), lambda i: (i,), memory_space=pltpu.SMEM),
            pl.BlockSpec((tb, D), lambda i: (i, 0)),
            pl.BlockSpec((tb, TOP_K), lambda i: (i, 0)),
            pl.BlockSpec((tb, pd), lambda i: (i, 0)),
            pl.BlockSpec(memory_space=pl.ANY),
            full((1, D)), full((D, D)), full((pd, D)), full((1, D)),
        ],
        out_specs=pl.BlockSpec((tb, D), lambda i: (i, 0)),
        out_shape=jax.ShapeDtypeStruct((N, D), F32),
        scratch_shapes=[pltpu.VMEM((TOP_K, tb, ys.shape[1]), ys.dtype), pltpu.SemaphoreType.DMA(())],
        compiler_params=_params(1), name="combine_ple_norm",
    )(dest_flat, h1, gates, p2, ys, g_ple, w_gate, w_proj, g_final)


def _pick(n, prefs):
    for t in prefs:
        if n % t == 0:
            return t
    return n


def _rope_tables(S):
    inv = ROPE_THETA ** (-jnp.arange(0, HEAD_DIM, 2, dtype=F32) / HEAD_DIM)
    ang = jnp.arange(S, dtype=F32)[:, None] * inv[None, :]
    cos = jnp.cos(ang)
    sin = jnp.sin(ang)
    reps = LANES // HEAD_DIM
    return jnp.tile(jnp.concatenate([cos, cos], -1), (1, reps)), jnp.tile(jnp.concatenate([sin, sin], -1), (1, reps))


def _rot_cols(w, n_heads):
    D = w.shape[0]
    w3 = w.reshape(D, n_heads, 2, HEAD_DIM // 2)
    return jnp.stack([-w3[:, :, 1], w3[:, :, 0]], axis=2).reshape(D, n_heads * HEAD_DIM)


def _layer(h, p, w_in, w_out, g_mix, g_ffn, glu_b, dw_w, dw_b, ln_g, ln_b, w_router, b_router,
           w_up, b_up, w_down, b_down, w_ple_proj, w_ple_gate, g_ple, g_final):
    B, S, D = h.shape
    N = B * S
    aw = ATTN_HEADS * HEAD_DIM
    iw = IDX_HEADS * IDX_DIM
    o = np.cumsum([0, aw, HEAD_DIM, HEAD_DIM, iw, IDX_DIM, IDX_HEADS])
    wq, wk, wv, wqi, wki, wwi, wcu = (w_in[:, o[0]:o[1]], w_in[:, o[1]:o[2]], w_in[:, o[2]:o[3]],
                                      w_in[:, o[3]:o[4]], w_in[:, o[4]:o[5]], w_in[:, o[5]:o[6]], w_in[:, o[6]:])
    rope_w = jnp.concatenate([wq, wqi, wk, wki], axis=1)
    rope_rot = jnp.concatenate([_rot_cols(wq, ATTN_HEADS), _rot_cols(wqi, IDX_HEADS),
                                _rot_cols(wk, 1), _rot_cols(wki, 1)], axis=1)
    w_all = jnp.concatenate([rope_w, rope_rot, wcu], axis=1).astype(BF16)
    w_wi_t = jnp.concatenate([wwi.T, jnp.zeros((SUBLANES - IDX_HEADS, D), F32)], axis=0).astype(BF16)
    w_v_t = jnp.concatenate([wv.T, jnp.zeros((LANES - HEAD_DIM, D), F32)], axis=0).astype(BF16)
    cos_t, sin_t = _rope_tables(S)

    tm = _pick(S, (512, 256, 128))
    q, qi, k, ki, vt, wit, conv = _inproj(h, g_mix[None], w_all, w_wi_t, w_v_t, cos_t, sin_t, glu_b[None], dw_w,
                                          dw_b[None], ln_g[None], ln_b[None], tm)
    qb = _pick(S, (128,))
    attn = _dsa(q, qi, wit, k, ki, vt, qb, tm)

    h1, xn, route, gates, cnt = _mix_router(h.reshape(N, D), attn, conv, w_out.astype(BF16), g_ffn[None],
                                            w_router.astype(BF16), b_router[None], tm)

    tmx = 512
    counts = cnt[0, :N_EXPERTS].astype(I32)
    padded = (counts + tmx - 1) // tmx * tmx
    cum = jnp.cumsum(padded)
    pstart = cum - padded
    n_blocks = (N * TOP_K + N_EXPERTS * (tmx - 1) + tmx - 1) // tmx
    n_rows = n_blocks * tmx
    dest = (pstart[route[:, :TOP_K]] + route[:, TOP_K:]).reshape(N * TOP_K)
    n_used = (cum[-1] // tmx).astype(I32)
    blk_start = jnp.minimum(jnp.arange(n_blocks, dtype=I32), n_used - 1) * tmx
    block_exp = jnp.minimum(jnp.sum((blk_start[:, None] >= cum[None, :]).astype(I32), axis=1), N_EXPERTS - 1)
    zspan = tmx + SUBLANES
    zstart = (jnp.minimum(pstart + counts, n_rows - zspan) // SUBLANES * SUBLANES).astype(I32)
    zstart = jnp.concatenate([zstart, n_used.reshape(1)])

    tb = _pick(N, (256,))
    xs = _dispatch(zstart, dest, xn, n_rows, tb, tmx, zspan)
    ys = _experts(block_exp, n_used.reshape(1), xs, w_up, b_up, w_down, b_down, tmx)
    out = _combine(dest, h1, gates, p.reshape(N, -1), ys, g_ple[None], w_ple_gate.astype(BF16),
                   w_ple_proj.astype(BF16), g_final[None], tb)
    return out.reshape(B, S, D)


def kernel(x, p, w_in, w_out, g_mix, g_ffn, conv_glu_b, conv_dw_w, conv_dw_b, conv_ln_g, conv_ln_b, w_router,
           b_router, w_up, b_up, w_down, b_down, w_ple_proj, w_ple_gate, g_ple, g_final):
    assert p.shape[0] == 1, "single trunk layer"
    return _layer(x, p[0], w_in[0], w_out[0], g_mix[0], g_ffn[0], conv_glu_b[0], conv_dw_w[0], conv_dw_b[0],
                  conv_ln_g[0], conv_ln_b[0], w_router[0], b_router[0], w_up[0], b_up[0], w_down[0], b_down[0],
                  w_ple_proj[0], w_ple_gate[0], g_ple[0], g_final)
```

```python
import functools

import jax
import jax.numpy as jnp
import numpy as np
from jax import lax
from jax.experimental import pallas as pl
from jax.experimental.pallas import tpu as pltpu

F32 = jnp.float32
BF16 = jnp.bfloat16
I32 = jnp.int32

ATTN_HEADS = 8
HEAD_DIM = 64
IDX_HEADS = 4
IDX_DIM = 64
TOPK_MAX = 256
CONV_KERNEL = 31
N_EXPERTS = 32
TOP_K = 4
SWIGLU_LIMIT = 7.0
SWIGLU_ALPHA = 1.702
ROPE_THETA = 10000.0
EPS = 1e-6

LANES = 128
SUBLANES = 8
CONV_HALO = 32
NEG_BIG = -1e30
INT_MIN = -2147483648
COUNT_CHAINS = 8
LOG2_E = 1.4426950408889634
HEADS_PER_GROUP = 8
VMEM_LIMIT = 56 * 1024 * 1024


def _params(n_axes):
    return pltpu.CompilerParams(dimension_semantics=("arbitrary",) * n_axes,
                                vmem_limit_bytes=VMEM_LIMIT)


def _rms(x, g):
    return x * lax.rsqrt(jnp.mean(x * x, axis=-1, keepdims=True) + EPS) * g


def _pack_bf16_pairs(x):
    c = x.shape[1] // 2
    lo = pltpu.bitcast(x[:, :c].astype(BF16).astype(F32), I32)
    hi = pltpu.bitcast(x[:, c:].astype(BF16).astype(F32), I32)
    return lax.shift_right_logical(lo, 16) | (hi & jnp.int32(-65536))


def _unpack_bf16_pairs(u):
    lo = pltpu.bitcast(lax.shift_left(u, 16), F32)
    hi = pltpu.bitcast(u & jnp.int32(-65536), F32)
    return jnp.concatenate([lo, hi], axis=1)


def _dot_t(a, b):
    return lax.dot_general(a, b, (((1,), (1,)), ((), ())), preferred_element_type=F32)


def _inproj_kernel(x_ref, g_ref, w_ref, wwi_ref, wvt_ref, cos_ref, sin_ref, glub_ref, dww_ref, dwb_ref, lng_ref,
                   lnb_ref, q_ref, qi_ref, k_ref, ki_ref, vt_ref, wit_ref, conv_ref, hbuf, hsh, *, tm, conv_w, row_blk):
    t = pl.program_id(1)
    xn = _rms(x_ref[...], g_ref[...]).astype(BF16)
    u = jnp.dot(xn, w_ref[...], preferred_element_type=F32)
    rope_w = (ATTN_HEADS + IDX_HEADS + 2) * HEAD_DIM
    cos = cos_ref[...]
    sin = sin_ref[...]
    r = [u[:, c * LANES:(c + 1) * LANES] * cos + u[:, rope_w + c * LANES:rope_w + (c + 1) * LANES] * sin
         for c in range(rope_w // LANES)]
    for h in range(ATTN_HEADS):
        blk = r[h // 2][:, (h % 2) * HEAD_DIM:(h % 2 + 1) * HEAD_DIM]
        q_ref[h] = (blk * (HEAD_DIM ** -0.5 * LOG2_E)).astype(BF16)
    for h in range(IDX_HEADS):
        blk = r[ATTN_HEADS // 2 + h // 2][:, (h % 2) * IDX_DIM:(h % 2 + 1) * IDX_DIM]
        qi_ref[h] = blk.astype(BF16)
    last = r[(ATTN_HEADS + IDX_HEADS) // 2]
    k_ref[...] = last[:, :HEAD_DIM].astype(BF16)
    ki_ref[...] = last[:, HEAD_DIM:].astype(BF16)
    vt = _dot_t(wvt_ref[...], xn)
    vrow = lax.broadcasted_iota(I32, vt.shape, 0)
    vt_ref[...] = jnp.where(vrow == HEAD_DIM, 1.0, vt).astype(BF16)
    wit_ref[...] = _dot_t(wwi_ref[...], xn)

    cu = u[:, 2 * rope_w:] + glub_ref[...]
    hh = cu[:, :conv_w] * jax.nn.sigmoid(cu[:, conv_w:])

    @pl.when(t == 0)
    def _():
        hbuf[0:CONV_HALO, :] = jnp.zeros((CONV_HALO, conv_w), F32)

    hbuf[CONV_HALO:CONV_HALO + tm, :] = hh
    base = CONV_HALO - (CONV_KERNEL - 1)
    for b in range(SUBLANES):
        n_al = tm + SUBLANES * ((CONV_KERNEL - 1 - b) // SUBLANES)
        hsh[b, 0:n_al, :] = hbuf[base + b:base + b + n_al, :]
    for rb in range(tm // row_blk):
        acc = jnp.zeros((row_blk, conv_w), F32)
        for j in range(CONV_KERNEL):
            a, b = divmod(j, SUBLANES)
            s0 = rb * row_blk + SUBLANES * a
            acc = acc + hsh[b, s0:s0 + row_blk, :] * dww_ref[j:j + 1, :]
        y = acc + dwb_ref[...]
        mu = jnp.mean(y, axis=-1, keepdims=True)
        yc = y - mu
        var = jnp.mean(yc * yc, axis=-1, keepdims=True)
        z = yc * lax.rsqrt(var + EPS) * lng_ref[...] + lnb_ref[...]
        conv_ref[rb * row_blk:(rb + 1) * row_blk, :] = (z * jax.nn.sigmoid(z)).astype(BF16)
    hbuf[0:CONV_HALO, :] = hbuf[tm:tm + CONV_HALO, :]


def _inproj(x, g_mix, w_all, w_wi_t, w_v_t, cos_t, sin_t, glu_b, dw_w, dw_b, ln_g, ln_b, tm):
    B, S, D = x.shape
    conv_w = dw_w.shape[1]
    nt = S // tm
    wtot = w_all.shape[1]
    row_blk = 64 if tm % 64 == 0 else tm
    kern = functools.partial(_inproj_kernel, tm=tm, conv_w=conv_w, row_blk=row_blk)
    full = lambda shape: pl.BlockSpec(shape, lambda b, t: (0,) * len(shape))
    out_shape = (
        jax.ShapeDtypeStruct((B, ATTN_HEADS, S, HEAD_DIM), BF16),
        jax.ShapeDtypeStruct((B, IDX_HEADS, S, IDX_DIM), BF16),
        jax.ShapeDtypeStruct((B, S, HEAD_DIM), BF16),
        jax.ShapeDtypeStruct((B, S, IDX_DIM), BF16),
        jax.ShapeDtypeStruct((B, nt, LANES, tm), BF16),
        jax.ShapeDtypeStruct((B, SUBLANES, S), F32),
        jax.ShapeDtypeStruct((B, S, conv_w), BF16),
    )
    out_specs = (
        pl.BlockSpec((None, ATTN_HEADS, tm, HEAD_DIM), lambda b, t: (b, 0, t, 0)),
        pl.BlockSpec((None, IDX_HEADS, tm, IDX_DIM), lambda b, t: (b, 0, t, 0)),
        pl.BlockSpec((None, tm, HEAD_DIM), lambda b, t: (b, t, 0)),
        pl.BlockSpec((None, tm, IDX_DIM), lambda b, t: (b, t, 0)),
        pl.BlockSpec((None, None, LANES, tm), lambda b, t: (b, t, 0, 0)),
        pl.BlockSpec((None, SUBLANES, tm), lambda b, t: (b, 0, t)),
        pl.BlockSpec((None, tm, conv_w), lambda b, t: (b, t, 0)),
    )
    in_specs = [
        pl.BlockSpec((None, tm, D), lambda b, t: (b, t, 0)),
        full((1, D)),
        full((D, wtot)),
        full((SUBLANES, D)),
        full((LANES, D)),
        pl.BlockSpec((tm, LANES), lambda b, t: (t, 0)),
        pl.BlockSpec((tm, LANES), lambda b, t: (t, 0)),
        full((1, 2 * conv_w)),
        full((CONV_KERNEL, conv_w)),
        full((1, conv_w)),
        full((1, conv_w)),
        full((1, conv_w)),
    ]
    return pl.pallas_call(
        kern, grid=(B, nt), in_specs=in_specs, out_specs=out_specs, out_shape=out_shape,
        scratch_shapes=[pltpu.VMEM((CONV_HALO + tm, conv_w), F32),
                        pltpu.VMEM((SUBLANES, tm + SUBLANES * ((CONV_KERNEL - 1) // SUBLANES), conv_w), F32)],
        compiler_params=_params(2), name="inproj_rope_conv",
    )(x, g_mix, w_all, w_wi_t, w_v_t, cos_t, sin_t, glu_b, dw_w, dw_b, ln_g, ln_b)


def _count(pred, key_scr, n_chunks, qb, sc):
    lanes_acc = COUNT_CHAINS * SUBLANES
    acc = jnp.zeros((lanes_acc, qb), F32)
    for c in range(n_chunks):
        keys3 = key_scr[c * sc:(c + 1) * sc, :].reshape(sc // SUBLANES, SUBLANES, qb)
        ones = jnp.where(pred(keys3, c * sc), 1.0, 0.0).reshape(sc // lanes_acc, lanes_acc, qb)
        acc = acc + jnp.sum(ones, axis=0)
    acc = jnp.sum(acc.reshape(COUNT_CHAINS, SUBLANES, qb), axis=0)
    return jnp.broadcast_to(jnp.sum(acc, axis=0, keepdims=True), (SUBLANES, qb))


def _select(key_scr, n_chunks, k_in_chunk3, *, qb, sc, n_sel, idx_bits, seq):
    n_sel_f = jnp.float32(n_sel)
    count = functools.partial(_count, key_scr=key_scr, n_chunks=n_chunks, qb=qb, sc=sc)

    cnt0 = count(lambda k3, off: k3 >= 0)
    thr = jnp.where(cnt0 >= n_sel_f, jnp.int32(0), jnp.int32(INT_MIN))

    def bit_body(i, thr):
        cand = thr | lax.shift_left(jnp.int32(1), jnp.int32(30) - i)
        cnt = count(lambda k3, off: k3 >= cand[None])
        return jnp.where(cnt >= n_sel_f, cand, thr)

    thr = lax.fori_loop(0, 31, bit_body, thr)

    cnt_ge = count(lambda k3, off: k3 >= thr[None])

    def tie_break():
        need = n_sel_f - count(lambda k3, off: k3 > thr[None])

        def idx_body(i, jj):
            cand = jj | lax.shift_left(jnp.int32(1), jnp.int32(idx_bits - 1) - i)
            cnt = count(lambda k3, off: jnp.logical_and(k3 == thr[None], off + k_in_chunk3 < cand[None]))
            return jnp.where(cnt < need, cand, jj)

        return lax.fori_loop(0, idx_bits, idx_body, jnp.zeros((SUBLANES, qb), I32))

    jmax = lax.cond(jnp.max(cnt_ge) > n_sel_f, tie_break, lambda: jnp.full((SUBLANES, qb), seq, I32))
    return thr, jmax


def _dsa_kernel(q_ref, qi_ref, wit_ref, k_ref, ki_ref, vt_ref, o_ref, key_scr, sel_scr, m_scr, acc_scr,
                *, qb, sc, n_sel, idx_bits, seq):
    j = pl.program_id(1)
    q_lo = j * qb
    nk = (q_lo + qb + sc - 1) // sc
    q_pos = q_lo + lax.broadcasted_iota(I32, (sc, qb), 1)
    k_in_chunk = lax.broadcasted_iota(I32, (sc, qb), 0)
    k_in_chunk3 = k_in_chunk.reshape(sc // SUBLANES, SUBLANES, qb)
    q_pos3 = q_pos.reshape(sc // SUBLANES, SUBLANES, qb)

    w_eff = wit_ref[...] * (IDX_DIM ** -0.5 * IDX_HEADS ** -0.5)

    def score_body(c, carry):
        off = pl.multiple_of(c * sc, sc)
        kc = ki_ref[pl.ds(off, sc), :]
        sco = jnp.zeros((sc, qb), F32)
        for h in range(IDX_HEADS):
            sco = sco + jnp.maximum(_dot_t(kc, qi_ref[h]), 0.0) * w_eff[h:h + 1, :]
        sco = jnp.where(sco == 0.0, 0.0, sco)
        sco = jnp.where(off + k_in_chunk <= q_pos, sco, -jnp.inf)
        bits = pltpu.bitcast(sco, I32)
        key_scr[pl.ds(off, sc), :] = jnp.where(bits < 0, bits ^ jnp.int32(0x7FFFFFFF), bits)
        return carry

    lax.fori_loop(0, nk, score_body, 0)

    for n_chunks in range(1, seq // sc + 1):
        @pl.when(nk == n_chunks)
        def _(n_chunks=n_chunks):
            t, jm = _select(key_scr, n_chunks, k_in_chunk3, qb=qb, sc=sc, n_sel=n_sel, idx_bits=idx_bits, seq=seq)
            sel_scr[0] = t
            sel_scr[1] = jm

    thr = sel_scr[0]
    jmax = sel_scr[1]

    n_pairs = ATTN_HEADS // HEADS_PER_GROUP
    pw = HEADS_PER_GROUP * qb
    m_scr[...] = jnp.full(m_scr.shape, NEG_BIG, F32)
    acc_scr[...] = jnp.zeros(acc_scr.shape, F32)

    def attn_body(c, carry):
        off = pl.multiple_of(c * sc, sc)
        kc = k_ref[pl.ds(off, sc), :]
        vt = vt_ref[c]
        keys3 = key_scr[pl.ds(off, sc), :].reshape(sc // SUBLANES, SUBLANES, qb)
        kidx3 = off + k_in_chunk3
        tie = jnp.logical_and(keys3 == thr[None], kidx3 <= jmax[None])
        sel = jnp.logical_and(jnp.logical_or(keys3 > thr[None], tie), kidx3 <= q_pos3)
        bias = jnp.where(sel, 0.0, NEG_BIG)
        bias2 = jnp.concatenate([bias] * HEADS_PER_GROUP, axis=-1)
        for hp in range(n_pairs):
            qp = q_ref[HEADS_PER_GROUP * hp:HEADS_PER_GROUP * (hp + 1)].reshape(pw, HEAD_DIM)
            s3 = _dot_t(kc, qp).reshape(sc // SUBLANES, SUBLANES, pw) + bias2
            mx = jnp.max(s3, axis=0)
            mx = jnp.broadcast_to(jnp.max(mx, axis=0, keepdims=True), mx.shape)
            m_old = m_scr[hp]
            m_new = jnp.maximum(m_old, mx)
            alpha = jnp.exp2(m_old - m_new)
            p = jnp.exp2(s3 - m_new[None]).reshape(sc, pw).astype(BF16)
            pv = jnp.dot(vt, p, preferred_element_type=F32)
            acc3 = acc_scr[hp].reshape(LANES // SUBLANES, SUBLANES, pw)
            acc_scr[hp] = (acc3 * alpha[None]).reshape(LANES, pw) + pv
            m_scr[hp] = m_new
        return carry

    lax.fori_loop(0, nk, attn_body, 0)
    for hp in range(n_pairs):
        acc = acc_scr[hp]
        out_t = acc[:HEAD_DIM, :] / acc[HEAD_DIM:HEAD_DIM + 1, :]
        o_ref[HEADS_PER_GROUP * hp:HEADS_PER_GROUP * (hp + 1)] = (
            out_t.T.reshape(HEADS_PER_GROUP, qb, HEAD_DIM).astype(BF16))


def _dsa(q, qi, wit, k, ki, vt, qb, sc):
    B, H, S, dh = q.shape
    n_sel = min(TOPK_MAX, S // 4)
    idx_bits = max(1, int(np.ceil(np.log2(S))))
    assert S % sc == 0 and sc % qb == 0 and qb % LANES == 0 and vt.shape[1:] == (S // sc, LANES, sc)
    kern = functools.partial(_dsa_kernel, qb=qb, sc=sc, n_sel=n_sel, idx_bits=idx_bits, seq=S)
    in_specs = [
        pl.BlockSpec((None, H, qb, dh), lambda b, j: (b, 0, j, 0)),
        pl.BlockSpec((None, IDX_HEADS, qb, IDX_DIM), lambda b, j: (b, 0, j, 0)),
        pl.BlockSpec((None, SUBLANES, qb), lambda b, j: (b, 0, j)),
        pl.BlockSpec((None, S, dh), lambda b, j: (b, 0, 0)),
        pl.BlockSpec((None, S, IDX_DIM), lambda b, j: (b, 0, 0)),
        pl.BlockSpec((None, S // sc, LANES, sc), lambda b, j: (b, 0, 0, 0)),
    ]
    return pl.pallas_call(
        kern, grid=(B, S // qb), in_specs=in_specs,
        out_specs=pl.BlockSpec((None, H, qb, dh), lambda b, j: (b, 0, j, 0)),
        out_shape=jax.ShapeDtypeStruct((B, H, S, dh), BF16),
        scratch_shapes=[
            pltpu.VMEM((S, qb), I32),
            pltpu.VMEM((2, SUBLANES, qb), I32),
            pltpu.VMEM((H // HEADS_PER_GROUP, SUBLANES, HEADS_PER_GROUP * qb), F32),
            pltpu.VMEM((H // HEADS_PER_GROUP, LANES, HEADS_PER_GROUP * qb), F32),
        ],
        compiler_params=_params(2), name="dsa_attention",
    )(q, qi, wit, k, ki, vt)


def _mix_router_kernel(x_ref, attn_ref, conv_ref, wout_ref, g_ref, wr_ref, br_ref,
                       h1_ref, xn_ref, route_ref, gate_ref, cnt_ref, carry, *, tm):
    i = pl.program_id(0)

    @pl.when(i == 0)
    def _():
        carry[...] = jnp.zeros(carry.shape, F32)

    attn_w = ATTN_HEADS * HEAD_DIM
    acc = x_ref[...] + jnp.dot(conv_ref[...], wout_ref[attn_w:, :], preferred_element_type=F32)
    for h in range(ATTN_HEADS):
        acc = acc + jnp.dot(attn_ref[h], wout_ref[h * HEAD_DIM:(h + 1) * HEAD_DIM, :],
                            preferred_element_type=F32)
    h1_ref[...] = acc
    xn = _rms(acc, g_ref[...])
    xn_ref[...] = _pack_bf16_pairs(xn)
    logits = jnp.dot(xn.astype(BF16), wr_ref[...], preferred_element_type=F32) + br_ref[...]

    ne = logits.shape[1]
    eid = lax.broadcasted_iota(I32, (tm, ne), 1)
    work = logits
    vals, hots = [], []
    for _ in range(TOP_K):
        m = jnp.max(work, axis=-1, keepdims=True)
        e = jnp.min(jnp.where(work == m, eid, ne), axis=-1, keepdims=True)
        hot = eid == e
        work = jnp.where(hot, -jnp.inf, work)
        vals.append(m)
        hots.append(hot)
    ex = [jnp.exp(v - vals[0]) for v in vals]
    den = ex[0] + ex[1] + ex[2] + ex[3]

    multi = jnp.zeros((tm, ne), F32)
    for hot in hots:
        multi = multi + jnp.where(hot, 1.0, 0.0)
    ri = lax.broadcasted_iota(I32, (tm, tm), 0)
    ci = lax.broadcasted_iota(I32, (tm, tm), 1)
    tri = jnp.where(ci < ri, 1.0, 0.0).astype(BF16)
    before = jnp.dot(tri, multi.astype(BF16), preferred_element_type=F32) + carry[0:1, :ne]
    carry[0:1, :ne] = carry[0:1, :ne] + jnp.sum(multi, axis=0, keepdims=True)
    cnt_ref[...] = carry[...]

    lane8 = lax.broadcasted_iota(I32, (tm, 2 * TOP_K), 1)
    lane4 = lax.broadcasted_iota(I32, (tm, TOP_K), 1)
    route = jnp.zeros((tm, 2 * TOP_K), I32)
    gates = jnp.zeros((tm, TOP_K), F32)
    for kk in range(TOP_K):
        e = jnp.sum(jnp.where(hots[kk], eid, 0), axis=-1, keepdims=True)
        rk = jnp.sum(jnp.where(hots[kk], before, 0.0), axis=-1, keepdims=True).astype(I32)
        route = route + jnp.where(lane8 == kk, e, 0) + jnp.where(lane8 == TOP_K + kk, rk, 0)
        gates = gates + jnp.where(lane4 == kk, ex[kk] / den, 0.0)
    route_ref[...] = route
    gate_ref[...] = gates


def _mix_router(x2, attn, conv, w_out, g_ffn, w_router, b_router, tm):
    N, D = x2.shape
    B, H, S, dh = attn.shape
    nt = S // tm
    ne = w_router.shape[1]
    cw = conv.shape[-1]
    kern = functools.partial(_mix_router_kernel, tm=tm)
    full = lambda shape: pl.BlockSpec(shape, lambda i: (0,) * len(shape))
    in_specs = [
        pl.BlockSpec((tm, D), lambda i: (i, 0)),
        pl.BlockSpec((None, H, tm, dh), lambda i: (i // nt, 0, i % nt, 0)),
        pl.BlockSpec((None, tm, cw), lambda i: (i // nt, i % nt, 0)),
        full(w_out.shape), full((1, D)), full((D, ne)), full((1, ne)),
    ]
    out_shape = (
        jax.ShapeDtypeStruct((N, D), F32),
        jax.ShapeDtypeStruct((N, D // 2), I32),
        jax.ShapeDtypeStruct((N, 2 * TOP_K), I32),
        jax.ShapeDtypeStruct((N, TOP_K), F32),
        jax.ShapeDtypeStruct((SUBLANES, LANES), F32),
    )
    out_specs = (
        pl.BlockSpec((tm, D), lambda i: (i, 0)),
        pl.BlockSpec((tm, D // 2), lambda i: (i, 0)),
        pl.BlockSpec((tm, 2 * TOP_K), lambda i: (i, 0)),
        pl.BlockSpec((tm, TOP_K), lambda i: (i, 0)),
        pl.BlockSpec((SUBLANES, LANES), lambda i: (0, 0)),
    )
    return pl.pallas_call(
        kern, grid=(N // tm,), in_specs=in_specs, out_specs=out_specs, out_shape=out_shape,
        scratch_shapes=[pltpu.VMEM((SUBLANES, LANES), F32)],
        compiler_params=_params(1), name="outproj_router",
    )(x2, attn, conv, w_out, g_ffn, w_router, b_router)


def _row_copy(src_ref, src_row, dst_ref, dst_row, sem):
    return pltpu.make_async_copy(src_ref.at[pl.ds(src_row, 1)], dst_ref.at[pl.ds(dst_row, 1)], sem)


def _dispatch_kernel(zstart_ref, dest_ref, x_ref, out_ref, zbuf, sem, zsem, *, tb, tmx, n_blocks):
    zspan = zbuf.shape[0]

    @pl.when(pl.program_id(0) == 0)
    def _():
        zbuf[...] = jnp.zeros(zbuf.shape, zbuf.dtype)
        for e in range(N_EXPERTS):
            z0 = pl.multiple_of(zstart_ref[e], SUBLANES)
            pltpu.make_async_copy(zbuf, out_ref.at[pl.ds(z0, zspan)], zsem).start()
        for e in range(N_EXPERTS):
            pltpu.make_async_copy(zbuf, out_ref.at[pl.ds(0, zspan)], zsem).wait()

        def tail_copy(i):
            return pltpu.make_async_copy(zbuf.at[pl.ds(0, tmx)], out_ref.at[pl.ds(pl.multiple_of(i * tmx, tmx), tmx)],
                                         zsem)

        def tail_start(i, c):
            tail_copy(i).start()
            return c

        def tail_wait(i, c):
            tail_copy(i).wait()
            return c

        lax.fori_loop(zstart_ref[N_EXPERTS], n_blocks, tail_start, 0)
        lax.fori_loop(zstart_ref[N_EXPERTS], n_blocks, tail_wait, 0)

    def start(r, c):
        for kk in range(TOP_K):
            _row_copy(x_ref, r, out_ref, dest_ref[r * TOP_K + kk], sem).start()
        return c

    lax.fori_loop(0, tb, start, 0)

    def wait(r, c):
        for kk in range(TOP_K):
            _row_copy(x_ref, 0, out_ref, 0, sem).wait()
        return c

    lax.fori_loop(0, tb, wait, 0)


def _dispatch(zstart, dest_flat, xn, n_rows, tb, tmx, zspan):
    N, D = xn.shape
    kern = functools.partial(_dispatch_kernel, tb=tb, tmx=tmx, n_blocks=n_rows // tmx)
    grid_spec = pltpu.PrefetchScalarGridSpec(
        num_scalar_prefetch=1, grid=(N // tb,),
        in_specs=[
            pl.BlockSpec((tb * TOP_K,), lambda i, zs: (i,), memory_space=pltpu.SMEM),
            pl.BlockSpec((tb, D), lambda i, zs: (i, 0)),
        ],
        out_specs=pl.BlockSpec(memory_space=pl.ANY),
        scratch_shapes=[pltpu.VMEM((zspan, D), xn.dtype), pltpu.SemaphoreType.DMA(()),
                        pltpu.SemaphoreType.DMA(())],
    )
    return pl.pallas_call(
        kern, grid_spec=grid_spec,
        out_shape=jax.ShapeDtypeStruct((n_rows, D), xn.dtype),
        compiler_params=_params(1), name="moe_dispatch",
    )(zstart, dest_flat, xn)


def _expert_kernel(bexp_ref, nused_ref, x_ref, wu_ref, bu_ref, wd_ref, bd_ref, y_ref, wu_bf, wd_bf):
    i = pl.program_id(0)
    used = i < nused_ref[0]

    @pl.when(jnp.logical_and(used, jnp.logical_or(i == 0, bexp_ref[i] != bexp_ref[jnp.maximum(i - 1, 0)])))
    def _():
        wu_bf[...] = wu_ref[...].astype(BF16)
        wd_bf[...] = wd_ref[...].astype(BF16)

    @pl.when(used)
    def _():
        de = wd_ref.shape[0]
        x = _unpack_bf16_pairs(x_ref[...]).astype(BF16)
        h = jnp.dot(x, wu_bf[...], preferred_element_type=F32) + bu_ref[...]
        gt = jnp.minimum(h[:, :de], SWIGLU_LIMIT)
        lin = jnp.clip(h[:, de:], -SWIGLU_LIMIT, SWIGLU_LIMIT)
        act = gt * jax.nn.sigmoid(SWIGLU_ALPHA * gt) * (lin + 1.0)
        y = jnp.dot(act.astype(BF16), wd_bf[...], preferred_element_type=F32) + bd_ref[...]
        y_ref[...] = _pack_bf16_pairs(y)

    @pl.when(i >= nused_ref[0])
    def _():
        y_ref[...] = jnp.zeros(y_ref.shape, y_ref.dtype)


def _experts(block_exp, n_used, xs, w_up, b_up, w_down, b_down, tmx):
    P, dp = xs.shape
    E, D, two_de = w_up.shape
    de = two_de // 2
    grid_spec = pltpu.PrefetchScalarGridSpec(
        num_scalar_prefetch=2, grid=(P // tmx,),
        in_specs=[
            pl.BlockSpec((tmx, dp), lambda i, be, nu: (jnp.minimum(i, nu[0] - 1), 0)),
            pl.BlockSpec((None, D, two_de), lambda i, be, nu: (be[i], 0, 0)),
            pl.BlockSpec((None, 1, two_de), lambda i, be, nu: (be[i], 0, 0)),
            pl.BlockSpec((None, de, D), lambda i, be, nu: (be[i], 0, 0)),
            pl.BlockSpec((None, 1, D), lambda i, be, nu: (be[i], 0, 0)),
        ],
        out_specs=pl.BlockSpec((tmx, dp), lambda i, be, nu: (i, 0)),
        scratch_shapes=[pltpu.VMEM((D, two_de), BF16), pltpu.VMEM((de, D), BF16)],
    )
    return pl.pallas_call(
        _expert_kernel, grid_spec=grid_spec,
        out_shape=jax.ShapeDtypeStruct((P, dp), I32),
        compiler_params=_params(1), name="moe_experts",
    )(block_exp, n_used, xs, w_up, b_up.reshape(E, 1, two_de), w_down, b_down.reshape(E, 1, D))


def _combine_kernel(dest_ref, h1_ref, gate_ref, p_ref, ys_ref, gple_ref, wg_ref, wp_ref, gfin_ref,
                    o_ref, gbuf, sem, *, tb):
    def start(r, c):
        for kk in range(TOP_K):
            _row_copy(ys_ref, dest_ref[r * TOP_K + kk], gbuf.at[kk], r, sem).start()
        return c

    lax.fori_loop(0, tb, start, 0)

    def wait(r, c):
        for kk in range(TOP_K):
            _row_copy(ys_ref, 0, gbuf.at[kk], 0, sem).wait()
        return c

    lax.fori_loop(0, tb, wait, 0)

    g = gate_ref[...]
    h2 = h1_ref[...]
    for kk in range(TOP_K):
        h2 = h2 + g[:, kk:kk + 1] * _unpack_bf16_pairs(gbuf[kk])
    z = jnp.dot(_rms(h2, gple_ref[...]).astype(BF16), wg_ref[...], preferred_element_type=F32)
    pe = jnp.dot(p_ref[...].astype(BF16), wp_ref[...], preferred_element_type=F32)
    h3 = h2 + pe * jax.nn.sigmoid(z)
    o_ref[...] = _rms(h3, gfin_ref[...])


def _combine(dest_flat, h1, gates, p2, ys, g_ple, w_gate, w_proj, g_final, tb):
    N, D = h1.shape
    pd = p2.shape[1]
    kern = functools.partial(_combine_kernel, tb=tb)
    full = lambda shape: pl.BlockSpec(shape, lambda i: (0,) * len(shape))
    return pl.pallas_call(
        kern, grid=(N // tb,),
        in_specs=[
            pl.BlockSpec((tb * TOP_K,), lambda i: (i,), memory_space=pltpu.SMEM),
            pl.BlockSpec((tb, D), lambda i: (i, 0)),
            pl.BlockSpec((tb, TOP_K), lambda i: (i, 0)),
            pl.BlockSpec((tb, pd), lambda i: (i, 0)),
            pl.BlockSpec(memory_space=pl.ANY),
            full((1, D)), full((D, D)), full((pd, D)), full((1, D)),
        ],
        out_specs=pl.BlockSpec((tb, D), lambda i: (i, 0)),
        out_shape=jax.ShapeDtypeStruct((N, D), F32),
        scratch_shapes=[pltpu.VMEM((TOP_K, tb, ys.shape[1]), ys.dtype), pltpu.SemaphoreType.DMA(())],
        compiler_params=_params(1), name="combine_ple_norm",
    )(dest_flat, h1, gates, p2, ys, g_ple, w_gate, w_proj, g_final)


def _pick(n, prefs):
    for t in prefs:
        if n % t == 0:
            return t
    return n


def _rope_tables(S):
    inv = ROPE_THETA ** (-jnp.arange(0, HEAD_DIM, 2, dtype=F32) / HEAD_DIM)
    ang = jnp.arange(S, dtype=F32)[:, None] * inv[None, :]
    cos = jnp.cos(ang)
    sin = jnp.sin(ang)
    reps = LANES // HEAD_DIM
    return jnp.tile(jnp.concatenate([cos, cos], -1), (1, reps)), jnp.tile(jnp.concatenate([sin, sin], -1), (1, reps))


def _rot_cols(w, n_heads):
    D = w.shape[0]
    w3 = w.reshape(D, n_heads, 2, HEAD_DIM // 2)
    return jnp.stack([-w3[:, :, 1], w3[:, :, 0]], axis=2).reshape(D, n_heads * HEAD_DIM)


def _layer(h, p, w_in, w_out, g_mix, g_ffn, glu_b, dw_w, dw_b, ln_g, ln_b, w_router, b_router,
           w_up, b_up, w_down, b_down, w_ple_proj, w_ple_gate, g_ple, g_final):
    B, S, D = h.shape
    N = B * S
    aw = ATTN_HEADS * HEAD_DIM
    iw = IDX_HEADS * IDX_DIM
    o = np.cumsum([0, aw, HEAD_DIM, HEAD_DIM, iw, IDX_DIM, IDX_HEADS])
    wq, wk, wv, wqi, wki, wwi, wcu = (w_in[:, o[0]:o[1]], w_in[:, o[1]:o[2]], w_in[:, o[2]:o[3]],
                                      w_in[:, o[3]:o[4]], w_in[:, o[4]:o[5]], w_in[:, o[5]:o[6]], w_in[:, o[6]:])
    rope_w = jnp.concatenate([wq, wqi, wk, wki], axis=1)
    rope_rot = jnp.concatenate([_rot_cols(wq, ATTN_HEADS), _rot_cols(wqi, IDX_HEADS),
                                _rot_cols(wk, 1), _rot_cols(wki, 1)], axis=1)
    w_all = jnp.concatenate([rope_w, rope_rot, wcu], axis=1).astype(BF16)
    w_wi_t = jnp.concatenate([wwi.T, jnp.zeros((SUBLANES - IDX_HEADS, D), F32)], axis=0).astype(BF16)
    w_v_t = jnp.concatenate([wv.T, jnp.zeros((LANES - HEAD_DIM, D), F32)], axis=0).astype(BF16)
    cos_t, sin_t = _rope_tables(S)

    tm = _pick(S, (512, 256, 128))
    q, qi, k, ki, vt, wit, conv = _inproj(h, g_mix[None], w_all, w_wi_t, w_v_t, cos_t, sin_t, glu_b[None], dw_w,
                                          dw_b[None], ln_g[None], ln_b[None], tm)
    qb = _pick(S, (128,))
    attn = _dsa(q, qi, wit, k, ki, vt, qb, tm)

    h1, xn, route, gates, cnt = _mix_router(h.reshape(N, D), attn, conv, w_out.astype(BF16), g_ffn[None],
                                            w_router.astype(BF16), b_router[None], tm)

    tmx = 512
    counts = cnt[0, :N_EXPERTS].astype(I32)
    padded = (counts + tmx - 1) // tmx * tmx
    cum = jnp.cumsum(padded)
    pstart = cum - padded
    n_blocks = (N * TOP_K + N_EXPERTS * (tmx - 1) + tmx - 1) // tmx
    n_rows = n_blocks * tmx
    dest = (pstart[route[:, :TOP_K]] + route[:, TOP_K:]).reshape(N * TOP_K)
    n_used = (cum[-1] // tmx).astype(I32)
    blk_start = jnp.minimum(jnp.arange(n_blocks, dtype=I32), n_used - 1) * tmx
    block_exp = jnp.minimum(jnp.sum((blk_start[:, None] >= cum[None, :]).astype(I32), axis=1), N_EXPERTS - 1)
    zspan = tmx + SUBLANES
    zstart = (jnp.minimum(pstart + counts, n_rows - zspan) // SUBLANES * SUBLANES).astype(I32)
    zstart = jnp.concatenate([zstart, n_used.reshape(1)])

    tb = _pick(N, (256,))
    xs = _dispatch(zstart, dest, xn, n_rows, tb, tmx, zspan)
    ys = _experts(block_exp, n_used.reshape(1), xs, w_up, b_up, w_down, b_down, tmx)
    out = _combine(dest, h1, gates, p.reshape(N, -1), ys, g_ple[None], w_ple_gate.astype(BF16),
                   w_ple_proj.astype(BF16), g_final[None], tb)
    return out.reshape(B, S, D)


def kernel(x, p, w_in, w_out, g_mix, g_ffn, conv_glu_b, conv_dw_w, conv_dw_b, conv_ln_g, conv_ln_b, w_router,
           b_router, w_up, b_up, w_down, b_down, w_ple_proj, w_ple_gate, g_ple, g_final):
    assert p.shape[0] == 1, "single trunk layer"
    return _layer(x, p[0], w_in[0], w_out[0], g_mix[0], g_ffn[0], conv_glu_b[0], conv_dw_w[0], conv_dw_b[0],
                  conv_ln_g[0], conv_ln_b[0], w_router[0], b_router[0], w_up[0], b_up[0], w_down[0], b_down[0],
                  w_ple_proj[0], w_ple_gate[0], g_ple[0], g_final)
```
